```python
import jax, jax.numpy as jnp
from jax import lax
import numpy as np

D_MODEL = 4096
BATCH = 32
SEQ = 256
DEPTH = 1
DEC_BATCH = 4
DEC_SEQ = 2048
PAST_LEN = 512

GRID_W = 64
SSD_EXPAND = 2
D_SSD = SSD_EXPAND * D_MODEL
SSD_HEAD_DIM = 64
SSD_HEADS = D_SSD // SSD_HEAD_DIM
SSD_GROUPS = 8
SSD_HPG = SSD_HEADS // SSD_GROUPS
SSD_STATE = 128
SSD_CHUNK = 128
CONV_W = 4
D_XBC = D_SSD + 2 * SSD_GROUPS * SSD_STATE
D_LRU = D_MODEL
LRU_BLOCKS = 16
LRU_BW = D_LRU // LRU_BLOCKS
LRU_C = 8.0
N_EXPERTS = 16
EC_CAPACITY_FACTOR = 2
D_FF = 10944
SPLIT_POINTS = (D_SSD, D_SSD + D_XBC, D_SSD + D_XBC + 2 * SSD_HEADS,
                D_SSD + D_XBC + 2 * SSD_HEADS + D_LRU, D_SSD + D_XBC + 2 * SSD_HEADS + 2 * D_LRU)
N_PROJ = D_SSD + D_XBC + 2 * SSD_HEADS + 2 * D_LRU + 2 * D_MODEL
N_MOD = 6
ALPHA = (2.0 * DEPTH) ** 0.25
BETA = (8.0 * DEPTH) ** -0.25
LN_EPS = 1e-5

kernel_name = 'hybrid_ssd_rglru_ec_dit_step'


def layer_norm(x, g, b):
    xf = x.astype(jnp.float32)
    mu = jnp.mean(xf, axis=-1, keepdims=True)
    var = jnp.mean(jnp.square(xf - mu), axis=-1, keepdims=True)
    return ((xf - mu) * lax.rsqrt(var + LN_EPS) * g + b).astype(x.dtype)


def conv_centred(x, w, bias):
    l = x.shape[1]
    k = w.shape[0]
    pl = (k - 1) // 2
    xp = jnp.pad(x, ((0, 0), (pl, k - 1 - pl), (0, 0)))
    return sum(xp[:, j:j + l] * w[j] for j in range(k)) + bias


def raster_to_colmajor(x):
    b, l, d = x.shape
    rows = l // GRID_W
    return x.reshape(b, rows, GRID_W, d).transpose(0, 2, 1, 3).reshape(b, l, d)


def colmajor_to_raster(x):
    b, l, d = x.shape
    rows = l // GRID_W
    return x.reshape(b, GRID_W, rows, d).transpose(0, 2, 1, 3).reshape(b, l, d)


def ssd_scan(x, dt, a_neg, bm, cm, h0):
    b, l = x.shape[:2]
    q = SSD_CHUNK
    nc = l // q
    xc = x.reshape(b, nc, q, SSD_GROUPS, SSD_HPG, SSD_HEAD_DIM)
    dtc = dt.reshape(b, nc, q, SSD_GROUPS, SSD_HPG)
    bc = bm.reshape(b, nc, q, SSD_GROUPS, SSD_STATE)
    cc = cm.reshape(b, nc, q, SSD_GROUPS, SSD_STATE)
    a_cum = jnp.cumsum((dtc * a_neg.reshape(SSD_GROUPS, SSD_HPG)).astype(jnp.float32), axis=2)
    causal = jnp.tril(jnp.ones((q, q), dtype=bool))[None, None, :, :, None, None]
    seg = a_cum[:, :, :, None] - a_cum[:, :, None, :]
    decay = jnp.exp(jnp.where(causal, seg, -jnp.inf))
    cb = jnp.einsum('bclgn,bcsgn->bclsg', cc, bc)
    scores = cb[..., None] * decay * dtc[:, :, None]
    y_diag = jnp.einsum('bclsgk,bcsgkp->bclgkp', scores, xc)
    decay_to_end = jnp.exp(a_cum[:, :, -1:] - a_cum)
    chunk_states = jnp.einsum('bcsgn,bcsgk,bcsgkp->bcgkpn', bc, decay_to_end * dtc, xc)
    chunk_decay = jnp.exp(a_cum[:, :, -1])

    def carry_step(h, inp):
        dec, st = inp
        return h * dec[..., None, None] + st, h

    h_init = h0.reshape(b, SSD_GROUPS, SSD_HPG, SSD_HEAD_DIM, SSD_STATE).astype(jnp.float32)
    h_final, h_prev = lax.scan(carry_step, h_init,
                               (jnp.moveaxis(chunk_decay, 1, 0), jnp.moveaxis(chunk_states, 1, 0)))
    h_prev = jnp.moveaxis(h_prev, 0, 1)
    y_off = jnp.einsum('bclgn,bcgkpn,bclgk->bclgkp', cc, h_prev, jnp.exp(a_cum))
    y = (y_diag + y_off).reshape(b, l, SSD_HEADS, SSD_HEAD_DIM).astype(x.dtype)
    return y, h_final.reshape(b, SSD_HEADS, SSD_HEAD_DIM, SSD_STATE).astype(x.dtype)


def rglru_scan(x, w_a, b_a, w_x, b_x, lam, h0):
    b, l, d = x.shape
    xb = x.reshape(b, l, LRU_BLOCKS, LRU_BW)
    r = jax.nn.sigmoid(jnp.einsum('blhi,hij->blhj', xb, w_a).reshape(b, l, d) + b_a)
    i = jax.nn.sigmoid(jnp.einsum('blhi,hij->blhj', xb, w_x).reshape(b, l, d) + b_x)
    log_a = (-LRU_C * jax.nn.softplus(-lam) * r).astype(jnp.float32)
    a = jnp.exp(log_a)
    u = jnp.sqrt(-jnp.expm1(2.0 * log_a)) * (i * x)
    u = u.at[:, 0].add(a[:, 0] * h0)

    def combine(left, right):
        a_l, u_l = left
        a_r, u_r = right
        return a_l * a_r, a_r * u_l + u_r

    _, h = lax.associative_scan(combine, (a, u), axis=1)
    return h.astype(x.dtype), h[:, -1].astype(x.dtype)


def mixer_block(h, p, s_ssd_f, s_ssd_b, s_lru_f, s_lru_b, is_latent):
    b, l, _ = h.shape
    proj = jnp.einsum('bld,dn->bln', h, p['w_in'])
    z, xbc, dt_raw, lru_gate, lru_x, merge_logits = jnp.split(proj, SPLIT_POINTS, axis=-1)

    xbc = jax.nn.silu(conv_centred(xbc, p['conv_xbc_w'], p['conv_xbc_b']))
    xs, bm, cm = jnp.split(xbc, [D_SSD, D_SSD + SSD_GROUPS * SSD_STATE], axis=-1)
    xs = xs.reshape(b, l, SSD_HEADS, SSD_HEAD_DIM)
    bm = bm.reshape(b, l, SSD_GROUPS, SSD_STATE)
    cm = cm.reshape(b, l, SSD_GROUPS, SSD_STATE)
    dt_f = jax.nn.softplus(dt_raw[..., :SSD_HEADS] + p['dt_bias_fwd'])
    dt_b = jax.nn.softplus(dt_raw[..., SSD_HEADS:] + p['dt_bias_bwd'])
    y_f, s_ssd_f = ssd_scan(xs, dt_f, -jnp.exp(p['a_log_fwd']), bm, cm, s_ssd_f)
    y_b, s_ssd_b = ssd_scan(jnp.flip(xs, 1), jnp.flip(dt_b, 1), -jnp.exp(p['a_log_bwd']),
                            jnp.flip(bm, 1), jnp.flip(cm, 1), s_ssd_b)
    y = y_f + jnp.flip(y_b, 1) + p['d_skip'][:, None] * xs
    y = y.reshape(b, l, D_SSD) * jax.nn.silu(z)
    yg = y.reshape(b, l, SSD_GROUPS, D_SSD // SSD_GROUPS).astype(jnp.float32)
    yg = yg * lax.rsqrt(jnp.mean(jnp.square(yg), axis=-1, keepdims=True) + LN_EPS)
    y = (yg.reshape(b, l, D_SSD) * p['ssd_norm_w']).astype(h.dtype)
    branch_ssd = jnp.einsum('bli,id->bld', y, p['w_proj_ssd'])

    xr = raster_to_colmajor(lru_x) if is_latent else lru_x
    xr = conv_centred(xr, p['conv_lru_w'], p['conv_lru_b'])
    h_f, s_lru_f = rglru_scan(xr, p['lru_wa_fwd'], p['lru_ba_fwd'], p['lru_wx_fwd'], p['lru_bx_fwd'],
                              p['lru_lam_fwd'], s_lru_f)
    h_b, s_lru_b = rglru_scan(jnp.flip(xr, 1), p['lru_wa_bwd'], p['lru_ba_bwd'], p['lru_wx_bwd'],
                              p['lru_bx_bwd'], p['lru_lam_bwd'], s_lru_b)
    hr = h_f + jnp.flip(h_b, 1)
    if is_latent:
        hr = colmajor_to_raster(hr)
    branch_lru = jnp.einsum('bli,id->bld', jax.nn.gelu(lru_gate) * hr, p['w_proj_lru'])

    g_ssd, g_lru = jnp.split(jax.nn.sigmoid(merge_logits), 2, axis=-1)
    merged = g_ssd * branch_ssd + g_lru * branch_lru
    out = jnp.einsum('bld,de->ble', merged, p['w_out'])
    return out, (s_ssd_f, s_ssd_b, s_lru_f, s_lru_b)


def expert_choice_ffn(h, w_router, w1, w3, w2):
    b, n, d = h.shape
    cap = EC_CAPACITY_FACTOR * n // N_EXPERTS
    affinity = jax.nn.softmax(jnp.einsum('bnd,de->bne', h, w_router).astype(jnp.float32), axis=-1)
    gate, idx = lax.top_k(jnp.swapaxes(affinity, 1, 2), cap)
    h_sel = jnp.take_along_axis(h, idx.reshape(b, N_EXPERTS * cap)[..., None], axis=1)
    h_sel = h_sel.reshape(b, N_EXPERTS, cap, d)
    u = jnp.einsum('becd,edf->becf', h_sel, w1)
    v = jnp.einsum('becd,edf->becf', h_sel, w3)
    out = jnp.einsum('becf,efd->becd', jax.nn.silu(u) * v, w2) * gate[..., None].astype(h.dtype)
    return jnp.zeros_like(h).at[jnp.arange(b)[:, None, None], idx].add(out)


def trunk_layer(x, mod, p, states, is_latent):
    shift1, scale1, gate1, shift2, scale2, gate2 = jnp.split(mod, N_MOD, axis=-1)
    h = x * (1.0 + scale1) + shift1
    mix, new_states = mixer_block(h, p, *states, is_latent)
    x = layer_norm(ALPHA * x + gate1 * mix, p['ln1_g'], p['ln1_b'])
    h = x * (1.0 + scale2) + shift2
    ffn = expert_choice_ffn(h, p['w_router'], p['w1'], p['w3'], p['w2'])
    x = layer_norm(ALPHA * x + gate2 * ffn, p['ln2_g'], p['ln2_b'])
    return x, new_states


def setup_inputs(seed: int = 0) -> dict:
    key = jax.random.key(seed)
    ks = iter(jax.random.split(key, 64))

    def nrm(shape, scale):
        return jax.random.normal(next(ks), shape, jnp.float32) * scale

    def unif(shape, lo, hi):
        return jax.random.uniform(next(ks), shape, jnp.float32, lo, hi)

    def dt_bias():
        dt = jnp.exp(unif((DEPTH, SSD_HEADS), float(np.log(1e-3)), float(np.log(1e-1))))
        return dt + jnp.log(-jnp.expm1(-dt))

    def lru_lam():
        a_base = unif((DEPTH, D_LRU), 0.9, 0.999) ** (1.0 / LRU_C)
        return jnp.log(a_base) - jnp.log1p(-a_base)

    s_d = D_MODEL ** -0.5
    return {
        'x_prompt': nrm((BATCH, SEQ, D_MODEL), 1.0),
        'x_sample': nrm((DEC_BATCH, DEC_SEQ, D_MODEL), 1.0),
        'state_ssd_fwd': nrm((DEC_BATCH, DEPTH, SSD_HEADS, SSD_HEAD_DIM, SSD_STATE), 0.1),
        'state_ssd_bwd': nrm((DEC_BATCH, DEPTH, SSD_HEADS, SSD_HEAD_DIM, SSD_STATE), 0.1),
        'state_lru_fwd': nrm((DEC_BATCH, DEPTH, D_LRU), 0.5),
        'state_lru_bwd': nrm((DEC_BATCH, DEPTH, D_LRU), 0.5),
        'c': nrm((DEC_BATCH, D_MODEL), 1.0),
        'c_ctx': nrm((D_MODEL,), 1.0),
        'w_mod': nrm((DEPTH, D_MODEL, N_MOD * D_MODEL), 0.5 * s_d),
        'b_mod': nrm((DEPTH, N_MOD * D_MODEL), 0.02),
        'w_in': nrm((DEPTH, D_MODEL, N_PROJ), s_d),
        'conv_xbc_w': nrm((DEPTH, CONV_W, D_XBC), CONV_W ** -0.5),
        'conv_xbc_b': nrm((DEPTH, D_XBC), 0.02),
        'dt_bias_fwd': dt_bias(),
        'dt_bias_bwd': dt_bias(),
        'a_log_fwd': jnp.log(unif((DEPTH, SSD_HEADS), 1.0, 16.0)),
        'a_log_bwd': jnp.log(unif((DEPTH, SSD_HEADS), 1.0, 16.0)),
        'd_skip': 1.0 + nrm((DEPTH, SSD_HEADS), 0.1),
        'ssd_norm_w': 1.0 + nrm((DEPTH, D_SSD), 0.02),
        'w_proj_ssd': nrm((DEPTH, D_SSD, D_MODEL), D_SSD ** -0.5),
        'conv_lru_w': nrm((DEPTH, CONV_W, D_LRU), CONV_W ** -0.5),
        'conv_lru_b': nrm((DEPTH, D_LRU), 0.02),
        'lru_wa_fwd': nrm((DEPTH, LRU_BLOCKS, LRU_BW, LRU_BW), LRU_BW ** -0.5),
        'lru_ba_fwd': nrm((DEPTH, D_LRU), 0.02),
        'lru_wx_fwd': nrm((DEPTH, LRU_BLOCKS, LRU_BW, LRU_BW), LRU_BW ** -0.5),
        'lru_bx_fwd': nrm((DEPTH, D_LRU), 0.02),
        'lru_lam_fwd': lru_lam(),
        'lru_wa_bwd': nrm((DEPTH, LRU_BLOCKS, LRU_BW, LRU_BW), LRU_BW ** -0.5),
        'lru_ba_bwd': nrm((DEPTH, D_LRU), 0.02),
        'lru_wx_bwd': nrm((DEPTH, LRU_BLOCKS, LRU_BW, LRU_BW), LRU_BW ** -0.5),
        'lru_bx_bwd': nrm((DEPTH, D_LRU), 0.02),
        'lru_lam_bwd': lru_lam(),
        'w_proj_lru': nrm((DEPTH, D_LRU, D_MODEL), D_LRU ** -0.5),
        'w_out': nrm((DEPTH, D_MODEL, D_MODEL), s_d * BETA),
        'ln1_g': 1.0 + nrm((DEPTH, D_MODEL), 0.02),
        'ln1_b': nrm((DEPTH, D_MODEL), 0.02),
        'w_router': nrm((DEPTH, D_MODEL, N_EXPERTS), s_d),
        'w1': nrm((DEPTH, N_EXPERTS, D_MODEL, D_FF), s_d),
        'w3': nrm((DEPTH, N_EXPERTS, D_MODEL, D_FF), s_d),
        'w2': nrm((DEPTH, N_EXPERTS, D_FF, D_MODEL), D_FF ** -0.5 * BETA),
        'ln2_g': 1.0 + nrm((DEPTH, D_MODEL), 0.02),
        'ln2_b': nrm((DEPTH, D_MODEL), 0.02),
    }


def reference(x_prompt, x_sample, state_ssd_fwd, state_ssd_bwd, state_lru_fwd, state_lru_bwd, c, c_ctx,
              w_mod, b_mod, w_in, conv_xbc_w, conv_xbc_b, dt_bias_fwd, dt_bias_bwd, a_log_fwd, a_log_bwd,
              d_skip, ssd_norm_w, w_proj_ssd, conv_lru_w, conv_lru_b,
              lru_wa_fwd, lru_ba_fwd, lru_wx_fwd, lru_bx_fwd, lru_lam_fwd,
              lru_wa_bwd, lru_ba_bwd, lru_wx_bwd, lru_bx_bwd, lru_lam_bwd,
              w_proj_lru, w_out, ln1_g, ln1_b, w_router, w1, w3, w2, ln2_g, ln2_b):
    weights = dict(w_mod=w_mod, b_mod=b_mod, w_in=w_in, conv_xbc_w=conv_xbc_w, conv_xbc_b=conv_xbc_b,
                   dt_bias_fwd=dt_bias_fwd, dt_bias_bwd=dt_bias_bwd, a_log_fwd=a_log_fwd, a_log_bwd=a_log_bwd,
                   d_skip=d_skip, ssd_norm_w=ssd_norm_w, w_proj_ssd=w_proj_ssd,
                   conv_lru_w=conv_lru_w, conv_lru_b=conv_lru_b,
                   lru_wa_fwd=lru_wa_fwd, lru_ba_fwd=lru_ba_fwd, lru_wx_fwd=lru_wx_fwd, lru_bx_fwd=lru_bx_fwd,
                   lru_lam_fwd=lru_lam_fwd, lru_wa_bwd=lru_wa_bwd, lru_ba_bwd=lru_ba_bwd, lru_wx_bwd=lru_wx_bwd,
                   lru_bx_bwd=lru_bx_bwd, lru_lam_bwd=lru_lam_bwd, w_proj_lru=w_proj_lru, w_out=w_out,
                   ln1_g=ln1_g, ln1_b=ln1_b, w_router=w_router, w1=w1, w3=w3, w2=w2, ln2_g=ln2_g, ln2_b=ln2_b)

    bp = x_prompt.shape[0]
    y_prompt = x_prompt
    ssd_f_list, ssd_b_list, lru_f_list, lru_b_list = [], [], [], []
    for layer in range(DEPTH):
        p = {k: v[layer] for k, v in weights.items()}
        mod_ctx = (jax.nn.silu(c_ctx) @ p['w_mod'] + p['b_mod'])[None, None]
        zero_states = (jnp.zeros((bp, SSD_HEADS, SSD_HEAD_DIM, SSD_STATE), x_prompt.dtype),
                       jnp.zeros((bp, SSD_HEADS, SSD_HEAD_DIM, SSD_STATE), x_prompt.dtype),
                       jnp.zeros((bp, D_LRU), x_prompt.dtype),
                       jnp.zeros((bp, D_LRU), x_prompt.dtype))
        y_prompt, (sf, sb, lf, lb) = trunk_layer(y_prompt, mod_ctx, p, zero_states, False)
        ssd_f_list.append(sf)
        ssd_b_list.append(sb)
        lru_f_list.append(lf)
        lru_b_list.append(lb)
    new_ssd_fwd = jnp.stack(ssd_f_list, axis=1)
    new_ssd_bwd = jnp.stack(ssd_b_list, axis=1)
    new_lru_fwd = jnp.stack(lru_f_list, axis=1)
    new_lru_bwd = jnp.stack(lru_b_list, axis=1)

    y_sample = x_sample
    for layer in range(DEPTH):
        p = {k: v[layer] for k, v in weights.items()}
        mod_lat = (jax.nn.silu(c) @ p['w_mod'] + p['b_mod'])[:, None]
        cached = (state_ssd_fwd[:, layer], state_ssd_bwd[:, layer], state_lru_fwd[:, layer], state_lru_bwd[:, layer])
        y_sample, _ = trunk_layer(y_sample, mod_lat, p, cached, True)

    return (y_prompt, y_sample, new_ssd_fwd, new_ssd_bwd, new_lru_fwd, new_lru_bwd)
```

```python
import functools

import jax
import jax.numpy as jnp
from jax import lax
from jax.experimental import pallas as pl
from jax.experimental.pallas import tpu as pltpu

F32 = jnp.float32
BF16 = jnp.bfloat16

LANES = 128
SUBLANES = 8
MXU_DIM = 256
VMEM_LIMIT = 56 * 1024 * 1024

GRID_W = 64
SSD_CHUNK = 128
LRU_C = 8.0
LN_EPS = 1e-5
EC_CAPACITY_FACTOR = 2
N_MOD = 6
MOD_ROWS = 8


def _pick(n, pref, align):
    best = None
    t = align
    while t <= min(n, pref):
        if n % t == 0:
            best = t
        t += align
    return best if best is not None else n


def _cparams(*sem):
    return pltpu.CompilerParams(dimension_semantics=sem, vmem_limit_bytes=VMEM_LIMIT)


def _resident(shape, index_map):
    return pl.BlockSpec(shape, index_map, pipeline_mode=pl.Buffered(1))


def _mod_row(i, tm, base, rows_per_mod):
    return base + (i * tm) // rows_per_mod


def _dot(a, b):
    return jnp.dot(a, b, preferred_element_type=F32)


def _mod_kernel(c_ref, w_ref, b_ref, o_ref):
    cs = jax.nn.silu(c_ref[...]).astype(BF16)
    o_ref[...] = _dot(cs, w_ref[...].astype(BF16)) + b_ref[...]


def _mod_call(cv, w_mod, b_mod):
    d, n = w_mod.shape
    tn = _pick(n, 512, LANES)
    return pl.pallas_call(
        _mod_kernel,
        grid=(n // tn,),
        in_specs=[pl.BlockSpec((MOD_ROWS, d), lambda j: (0, 0)),
                  pl.BlockSpec((d, tn), lambda j: (0, j)),
                  pl.BlockSpec((1, tn), lambda j: (0, j))],
        out_specs=pl.BlockSpec((MOD_ROWS, tn), lambda j: (0, j)),
        out_shape=jax.ShapeDtypeStruct((MOD_ROWS, n), F32),
        compiler_params=_cparams("parallel"),
        name="mod",
    )(cv, w_mod, b_mod)


def _modulate_kernel(x_ref, sh_ref, sc_ref, o_ref, *, tm, base, rpm):
    r = _mod_row(pl.program_id(0), tm, base, rpm)
    sc = sc_ref[pl.ds(r, 1), :]
    sh = sh_ref[pl.ds(r, 1), :]
    o_ref[...] = (x_ref[...] * (1.0 + sc) + sh).astype(o_ref.dtype)


def _modulate_call(x, mod, base, rpm, q_shift, q_scale):
    t, d = x.shape
    tm = _pick(t, 256, SUBLANES)
    assert rpm % tm == 0
    return pl.pallas_call(
        functools.partial(_modulate_kernel, tm=tm, base=base, rpm=rpm),
        grid=(t // tm,),
        in_specs=[pl.BlockSpec((tm, d), lambda i: (i, 0)),
                  pl.BlockSpec((MOD_ROWS, d), lambda i: (0, q_shift)),
                  pl.BlockSpec((MOD_ROWS, d), lambda i: (0, q_scale))],
        out_specs=pl.BlockSpec((tm, d), lambda i: (i, 0)),
        out_shape=jax.ShapeDtypeStruct((t, d), BF16),
        compiler_params=_cparams("parallel"),
        name="modulate",
    )(x, mod, mod)


def _proj_in_kernel(x_ref, w_ref, o_ref):
    o_ref[...] = _dot(x_ref[...], w_ref[...].astype(BF16))


def _proj_in_call(h, w):
    t, k = h.shape
    n = w.shape[1]
    tm = _pick(t, 2048, SUBLANES)
    tn = _pick(n, MXU_DIM, LANES)
    return pl.pallas_call(
        _proj_in_kernel,
        grid=(t // tm, n // tn),
        in_specs=[_resident((tm, k), lambda i, j: (i, 0)),
                  pl.BlockSpec((k, tn), lambda i, j: (0, j))],
        out_specs=pl.BlockSpec((tm, tn), lambda i, j: (i, j)),
        out_shape=jax.ShapeDtypeStruct((t, n), F32),
        compiler_params=_cparams("parallel", "arbitrary"),
        name="proj_in",
    )(h, w)


def _proj_ssd_kernel(y_ref, w_ref, lg_ref, o_ref):
    acc = _dot(y_ref[...], w_ref[...].astype(BF16))
    o_ref[...] = jax.nn.sigmoid(lg_ref[...]) * acc


def _proj_ssd_call(y, w, proj, logit_off):
    t, k = y.shape
    n = w.shape[1]
    tm = _pick(t, 1024, SUBLANES)
    tn = _pick(n, MXU_DIM, LANES)
    assert logit_off % tn == 0
    lo = logit_off // tn
    return pl.pallas_call(
        _proj_ssd_kernel,
        grid=(t // tm, n // tn),
        in_specs=[_resident((tm, k), lambda i, j: (i, 0)),
                  pl.BlockSpec((k, tn), lambda i, j: (0, j)),
                  pl.BlockSpec((tm, tn), lambda i, j: (i, lo + j))],
        out_specs=pl.BlockSpec((tm, tn), lambda i, j: (i, j)),
        out_shape=jax.ShapeDtypeStruct((t, n), F32),
        compiler_params=_cparams("parallel", "arbitrary"),
        name="proj_ssd",
    )(y, w, proj)


def _proj_lru_kernel(h_ref, w_ref, lg_ref, s_ref, o_ref):
    acc = _dot(h_ref[...], w_ref[...].astype(BF16))
    o_ref[...] = (s_ref[...] + jax.nn.sigmoid(lg_ref[...]) * acc).astype(o_ref.dtype)


def _proj_lru_call(hl, w, proj, logit_off, branch_ssd):
    t, k = hl.shape
    n = w.shape[1]
    tm = _pick(t, 2048, SUBLANES)
    tn = _pick(n, MXU_DIM, LANES)
    assert logit_off % tn == 0
    lo = logit_off // tn
    return pl.pallas_call(
        _proj_lru_kernel,
        grid=(t // tm, n // tn),
        in_specs=[_resident((tm, k), lambda i, j: (i, 0)),
                  pl.BlockSpec((k, tn), lambda i, j: (0, j)),
                  pl.BlockSpec((tm, tn), lambda i, j: (i, lo + j)),
                  pl.BlockSpec((tm, tn), lambda i, j: (i, j))],
        out_specs=pl.BlockSpec((tm, tn), lambda i, j: (i, j)),
        out_shape=jax.ShapeDtypeStruct((t, n), BF16),
        compiler_params=_cparams("parallel", "arbitrary"),
        name="proj_lru",
    )(hl, w, proj, branch_ssd)


def _proj_out_kernel(m_ref, w_ref, x_ref, g_ref, o_ref, *, tm, base, rpm, alpha):
    r = _mod_row(pl.program_id(0), tm, base, rpm)
    acc = _dot(m_ref[...], w_ref[...].astype(BF16))
    o_ref[...] = alpha * x_ref[...] + g_ref[pl.ds(r, 1), :] * acc


def _proj_out_call(merged, w, x, mod, base, rpm, q_gate, alpha):
    t, k = merged.shape
    n = w.shape[1]
    tm = _pick(min(t, rpm), 2048, SUBLANES)
    tn = _pick(n, MXU_DIM, LANES)
    gq = q_gate * (n // tn)
    return pl.pallas_call(
        functools.partial(_proj_out_kernel, tm=tm, base=base, rpm=rpm, alpha=alpha),
        grid=(t // tm, n // tn),
        in_specs=[_resident((tm, k), lambda i, j: (i, 0)),
                  pl.BlockSpec((k, tn), lambda i, j: (0, j)),
                  pl.BlockSpec((tm, tn), lambda i, j: (i, j)),
                  pl.BlockSpec((MOD_ROWS, tn), lambda i, j: (0, gq + j))],
        out_specs=pl.BlockSpec((tm, tn), lambda i, j: (i, j)),
        out_shape=jax.ShapeDtypeStruct((t, n), F32),
        compiler_params=_cparams("parallel", "arbitrary"),
        name="proj_out",
    )(merged, w, x, mod)


def _shift_rows(x, d, reverse, fill):
    n = x.shape[0]
    row = lax.broadcasted_iota(jnp.int32, x.shape, 0)
    if reverse:
        return jnp.where(row < n - d, pltpu.roll(x, n - d, 0), fill)
    return jnp.where(row >= d, pltpu.roll(x, d, 0), fill)


def _cumsum_rows(a, reverse):
    d = 1
    while d < a.shape[0]:
        a = a + _shift_rows(a, d, reverse, 0.0)
        d *= 2
    return a


def _linear_scan(a, u, reverse):
    d = 1
    while d < a.shape[0]:
        u = a * _shift_rows(u, d, reverse, 0.0) + u
        a = a * _shift_rows(a, d, reverse, 1.0)
        d *= 2
    return u


def _conv4(x, w, b):
    return (_shift_rows(x, 1, False, 0.0) * w[0:1, :] + x * w[1:2, :]
            + _shift_rows(x, 1, True, 0.0) * w[2:3, :] + _shift_rows(x, 2, True, 0.0) * w[3:4, :] + b)


def _conv_silu_kernel(x_ref, w_ref, b_ref, o_ref):
    o_ref[...] = jax.nn.silu(_conv4(x_ref[...], w_ref[...], b_ref[...]))


def _conv_silu_call(proj, w, b, col_off, nb, seq):
    width = w.shape[1]
    ct = _pick(width, 256 if seq > 512 else 512, LANES)
    assert col_off % ct == 0 and w.shape[0] == 4
    co = col_off // ct
    return pl.pallas_call(
        _conv_silu_kernel,
        grid=(nb, width // ct),
        in_specs=[pl.BlockSpec((seq, ct), lambda i, j: (i, co + j)),
                  pl.BlockSpec((4, ct), lambda i, j: (0, j)),
                  pl.BlockSpec((1, ct), lambda i, j: (0, j))],
        out_specs=pl.BlockSpec((seq, ct), lambda i, j: (i, j)),
        out_shape=jax.ShapeDtypeStruct((nb * seq, width), F32),
        compiler_params=_cparams("parallel", "parallel"),
        name="conv_xbc",
    )(proj, w, b)


def _ssd_chunk(reverse, g, xs_ref, b_ref, c_ref, dt_ref, dtb_ref, alog_ref, h_scr, *, hpg, hd, heads):
    q = SSD_CHUNK
    dt = jax.nn.softplus(dt_ref[...] + dtb_ref[0])
    a = dt * (-jnp.exp(alog_ref[0]))
    acum = _cumsum_rows(a, reverse)
    shift = jnp.where(g == 0, 0, heads - g * hpg)
    acum = pltpu.roll(acum, shift, 1)
    dt = pltpu.roll(dt, shift, 1)
    acum_t = acum.T
    dt_t = dt.T
    cb16 = c_ref[...].astype(BF16)
    cb = lax.dot_general(cb16, b_ref[...].astype(BF16), (((1,), (1,)), ((), ())),
                         preferred_element_type=F32)
    bt16 = b_ref[...].T.astype(BF16)
    li = lax.broadcasted_iota(jnp.int32, (q, q), 0)
    si = lax.broadcasted_iota(jnp.int32, (q, q), 1)
    mask = (si >= li) if reverse else (si <= li)
    last = 0 if reverse else q - 1
    h_prev = h_scr[...]
    y_off = _dot(cb16, h_prev.astype(BF16))
    pw = 2 * hd
    lo = lax.broadcasted_iota(jnp.int32, (q, pw), 1) < hd
    ys = []
    for j in range(hpg // 2):
        k0, k1 = 2 * j, 2 * j + 1
        xp = xs_ref[:, j * pw:(j + 1) * pw]

        def scores(k):
            seg = acum[:, k:k + 1] - acum_t[k:k + 1, :]
            dec = jnp.exp(jnp.where(mask, seg, -jnp.inf))
            return (cb * dec * dt_t[k:k + 1, :]).astype(BF16)

        s2 = jnp.concatenate([scores(k0), scores(k1)], axis=1)
        rhs = jnp.concatenate([jnp.where(lo, xp, 0.0), jnp.where(lo, 0.0, xp)], axis=0).astype(BF16)
        y_diag = _dot(s2, rhs)
        a_pair = jnp.where(lo, acum[:, k0:k0 + 1], acum[:, k1:k1 + 1])
        e_pair = jnp.exp(a_pair)
        ys.append(y_diag + y_off[:, j * pw:(j + 1) * pw] * e_pair)
        d_pair = jnp.where(lo, dt[:, k0:k0 + 1], dt[:, k1:k1 + 1])
        w_pair = jnp.exp(a_pair[last:last + 1, :] - a_pair) * d_pair
        st = _dot(bt16, (xp * w_pair).astype(BF16))
        h_scr[:, j * pw:(j + 1) * pw] = h_prev[:, j * pw:(j + 1) * pw] * e_pair[last:last + 1, :] + st
    return jnp.concatenate(ys, axis=1)


def _ssd_load_state(h0_ref, h_scr, hpg, hd):
    pw = 2 * hd
    for j in range(hpg // 2):
        pair = jnp.concatenate([h0_ref[0, 2 * j], h0_ref[0, 2 * j + 1]], axis=0)
        h_scr[:, j * pw:(j + 1) * pw] = pair.T


def _ssd_store_state(s_ref, h_scr, hpg, hd):
    pw = 2 * hd
    for j in range(hpg // 2):
        pair = h_scr[:, j * pw:(j + 1) * pw].T
        s_ref[0, 2 * j] = pair[:hd]
        s_ref[0, 2 * j + 1] = pair[hd:]


def _ssd_kernel(*refs, nc, hpg, hd, heads, has_state):
    if has_state:
        (xs_ref, b_ref, c_ref, dt_ref, z_ref, dtb_ref, alog_ref, dsk_ref, nw_ref, h0f_ref, h0b_ref,
         y_ref, sf_ref, sb_ref, yf_scr, h_scr) = refs
    else:
        (xs_ref, b_ref, c_ref, dt_ref, z_ref, dtb_ref, alog_ref, dsk_ref, nw_ref,
         y_ref, sf_ref, sb_ref, yf_scr, h_scr) = refs
        h0f_ref = h0b_ref = None
    g = pl.program_id(1)
    step = pl.program_id(2)
    q = SSD_CHUNK
    chunk = functools.partial(_ssd_chunk, g=g, xs_ref=xs_ref, b_ref=b_ref, c_ref=c_ref, dt_ref=dt_ref,
                              dtb_ref=dtb_ref, alog_ref=alog_ref, h_scr=h_scr, hpg=hpg, hd=hd, heads=heads)

    def init(h0_ref):
        if has_state:
            _ssd_load_state(h0_ref, h_scr, hpg, hd)
        else:
            h_scr[...] = jnp.zeros_like(h_scr)

    @pl.when(step == 0)
    def _():
        init(h0f_ref)

    @pl.when(step == nc)
    def _():
        init(h0b_ref)

    @pl.when(step < nc)
    def _():
        row0 = pl.multiple_of(step * q, q)
        yf_scr[pl.ds(row0, q), :] = chunk(False)

    @pl.when(step == nc - 1)
    def _():
        _ssd_store_state(sf_ref, h_scr, hpg, hd)

    @pl.when(step >= nc)
    def _():
        row0 = pl.multiple_of((2 * nc - 1 - step) * q, q)
        y = yf_scr[pl.ds(row0, q), :] + chunk(True) + dsk_ref[...] * xs_ref[...]
        y = y * jax.nn.silu(z_ref[...])
        y = y * lax.rsqrt(jnp.mean(jnp.square(y), axis=-1, keepdims=True) + LN_EPS)
        y_ref[...] = (y * nw_ref[...]).astype(y_ref.dtype)

    @pl.when(step == 2 * nc - 1)
    def _():
        _ssd_store_state(sb_ref, h_scr, hpg, hd)


def _ssd_call(xbc, proj, dt_off, dt_bias, a_log, d_skip_x, norm_w, nb, seq, d_ssd, groups, hd, nstate, h0):
    q = SSD_CHUNK
    nc = seq // q
    gw = d_ssd // groups
    heads = d_ssd // hd
    hpg = heads // groups
    assert seq % q == 0 and hpg % 2 == 0 and 2 * hd == LANES and nstate == LANES and heads == LANES
    assert dt_off % heads == 0
    dto = dt_off // heads

    def chunk_of(s):
        return jnp.where(s < nc, s, 2 * nc - 1 - s)

    def out_chunk(s):
        return jnp.where(s < nc, nc - 1, 2 * nc - 1 - s)

    in_specs = [
        pl.BlockSpec((q, gw), lambda b, g, s: (b * nc + chunk_of(s), g)),
        pl.BlockSpec((q, nstate), lambda b, g, s: (b * nc + chunk_of(s), d_ssd // nstate + g)),
        pl.BlockSpec((q, nstate), lambda b, g, s: (b * nc + chunk_of(s), d_ssd // nstate + groups + g)),
        pl.BlockSpec((q, heads), lambda b, g, s: (b * nc + chunk_of(s), dto + s // nc)),
        pl.BlockSpec((q, gw), lambda b, g, s: (b * nc + out_chunk(s), g)),
        pl.BlockSpec((1, 1, heads), lambda b, g, s: (s // nc, 0, 0)),
        pl.BlockSpec((1, 1, heads), lambda b, g, s: (s // nc, 0, 0)),
        pl.BlockSpec((1, gw), lambda b, g, s: (0, g)),
        pl.BlockSpec((1, gw), lambda b, g, s: (0, g)),
    ]
    args = [xbc, xbc, xbc, proj, proj, dt_bias, a_log, d_skip_x, norm_w]
    state_spec = pl.BlockSpec((1, hpg, hd, nstate), lambda b, g, s: (b, g, 0, 0))
    if h0 is not None:
        in_specs += [state_spec, state_spec]
        args += [h0[0], h0[1]]
    state_shape = jax.ShapeDtypeStruct((nb, heads, hd, nstate), F32)
    return pl.pallas_call(
        functools.partial(_ssd_kernel, nc=nc, hpg=hpg, hd=hd, heads=heads, has_state=h0 is not None),
        grid=(nb, groups, 2 * nc),
        in_specs=in_specs,
        out_specs=[pl.BlockSpec((q, gw), lambda b, g, s: (b * nc + out_chunk(s), g)), state_spec, state_spec],
        out_shape=[jax.ShapeDtypeStruct((nb * seq, d_ssd), BF16), state_shape, state_shape],
        scratch_shapes=[pltpu.VMEM((seq, gw), F32), pltpu.VMEM((nstate, gw), F32)],
        compiler_params=_cparams("parallel", "parallel", "arbitrary"),
        name="ssd",
    )(*args)


def _lru_dir(x, xb, wa_ref, wx_ref, ba_ref, bx_ref, lam_ref, h0, reverse):
    r = jax.nn.sigmoid(_dot(xb, wa_ref[0].astype(BF16)) + ba_ref[...])
    i = jax.nn.sigmoid(_dot(xb, wx_ref[0].astype(BF16)) + bx_ref[...])
    log_a = -LRU_C * jax.nn.softplus(-lam_ref[...]) * r
    a = jnp.exp(log_a)
    u = jnp.sqrt(-jnp.tanh(log_a) * (a * a + 1.0)) * (i * x)
    first = x.shape[0] - 1 if reverse else 0
    row = lax.broadcasted_iota(jnp.int32, x.shape, 0)
    u = u + jnp.where(row == first, a * h0, 0.0)
    return _linear_scan(a, u, reverse)


def _lru_kernel(x_ref, cw_ref, cb_ref, waf_ref, wxf_ref, baf_ref, bxf_ref, lamf_ref,
                wab_ref, wxb_ref, bab_ref, bxb_ref, lamb_ref, h0f_ref, h0b_ref,
                hr_ref, sf_ref, sb_ref):
    x = _conv4(x_ref[...], cw_ref[...], cb_ref[...])
    xb = x.astype(BF16)
    n = x.shape[0]
    h_f = _lru_dir(x, xb, waf_ref, wxf_ref, baf_ref, bxf_ref, lamf_ref, h0f_ref[0], False)
    h_b = _lru_dir(x, xb, wab_ref, wxb_ref, bab_ref, bxb_ref, lamb_ref, h0b_ref[0], True)
    hr_ref[...] = h_f + h_b
    sf_ref[0] = h_f[n - 1:n, :]
    sb_ref[0] = h_b[0:1, :]


def _lru_call(xsrc, col_off, p, h0f, h0b, nb, seq):
    blocks, bw, _ = p['lru_wa_fwd'].shape
    d = blocks * bw
    assert col_off % bw == 0
    co = col_off // bw
    vec = pl.BlockSpec((1, bw), lambda b, j: (0, j))
    wblk = pl.BlockSpec((1, bw, bw), lambda b, j: (j, 0, 0))
    st = pl.BlockSpec((1, 1, bw), lambda b, j: (b, 0, j))
    st_shape = jax.ShapeDtypeStruct((nb, 1, d), F32)
    return pl.pallas_call(
        _lru_kernel,
        grid=(nb, blocks),
        in_specs=[pl.BlockSpec((seq, bw), lambda b, j: (b, co + j)),
                  pl.BlockSpec((4, bw), lambda b, j: (0, j)), vec,
                  wblk, wblk, vec, vec, vec, wblk, wblk, vec, vec, vec, st, st],
        out_specs=[pl.BlockSpec((seq, bw), lambda b, j: (b, j)), st, st],
        out_shape=[jax.ShapeDtypeStruct((nb * seq, d), F32), st_shape, st_shape],
        compiler_params=_cparams("parallel", "parallel"),
        name="rglru",
    )(xsrc, p['conv_lru_w'], p['conv_lru_b'],
      p['lru_wa_fwd'], p['lru_wx_fwd'], p['lru_ba_fwd'], p['lru_bx_fwd'], p['lru_lam_fwd'],
      p['lru_wa_bwd'], p['lru_wx_bwd'], p['lru_ba_bwd'], p['lru_bx_bwd'], p['lru_lam_bwd'],
      h0f, h0b)


def _gate_mul_kernel(g_ref, h_ref, o_ref):
    o_ref[...] = (jax.nn.gelu(g_ref[...]) * h_ref[...]).astype(o_ref.dtype)


def _gate_mul_call(proj, gate_off, hr):
    t, d = hr.shape
    tm = _pick(t, 1024, SUBLANES)
    tn = _pick(d, MXU_DIM, LANES)
    assert gate_off % tn == 0
    go = gate_off // tn
    return pl.pallas_call(
        _gate_mul_kernel,
        grid=(t // tm, d // tn),
        in_specs=[pl.BlockSpec((tm, tn), lambda i, j: (i, go + j)),
                  pl.BlockSpec((tm, tn), lambda i, j: (i, j))],
        out_specs=pl.BlockSpec((tm, tn), lambda i, j: (i, j)),
        out_shape=jax.ShapeDtypeStruct((t, d), BF16),
        compiler_params=_cparams("parallel", "parallel"),
        name="lru_gate",
    )(proj, hr)


def _layer_norm(v, g, b):
    mu = jnp.mean(v, axis=-1, keepdims=True)
    var = jnp.mean(jnp.square(v - mu), axis=-1, keepdims=True)
    return (v - mu) * lax.rsqrt(var + LN_EPS) * g + b


def _ln_router_kernel(v_ref, g_ref, b_ref, sh_ref, sc_ref, wr_ref, x1_ref, h2_ref, aff_ref,
                      *, tm, base, rpm, n_exp):
    r = _mod_row(pl.program_id(0), tm, base, rpm)
    x1 = _layer_norm(v_ref[...], g_ref[...], b_ref[...])
    x1_ref[...] = x1
    h2 = (x1 * (1.0 + sc_ref[pl.ds(r, 1), :]) + sh_ref[pl.ds(r, 1), :]).astype(BF16)
    h2_ref[...] = h2
    logits = _dot(h2, wr_ref[...].astype(BF16))
    lane = lax.broadcasted_iota(jnp.int32, logits.shape, 1)
    logits = jnp.where(lane < n_exp, logits, -jnp.inf)
    e = jnp.exp(logits - jnp.max(logits, axis=-1, keepdims=True))
    aff_ref[...] = e / jnp.sum(e, axis=-1, keepdims=True)


def _ln_router_call(v, ln_g, ln_b, mod, base, rpm, q_shift, q_scale, w_router_pad, n_exp):
    t, d = v.shape
    tm = _pick(t, 256, SUBLANES)
    assert rpm % tm == 0
    row = pl.BlockSpec((tm, d), lambda i: (i, 0))
    vec = pl.BlockSpec((1, d), lambda i: (0, 0))
    return pl.pallas_call(
        functools.partial(_ln_router_kernel, tm=tm, base=base, rpm=rpm, n_exp=n_exp),
        grid=(t // tm,),
        in_specs=[row, vec, vec,
                  pl.BlockSpec((MOD_ROWS, d), lambda i: (0, q_shift)),
                  pl.BlockSpec((MOD_ROWS, d), lambda i: (0, q_scale)),
                  pl.BlockSpec((d, LANES), lambda i: (0, 0))],
        out_specs=[row, row, pl.BlockSpec((tm, LANES), lambda i: (i, 0))],
        out_shape=[jax.ShapeDtypeStruct((t, d), F32), jax.ShapeDtypeStruct((t, d), BF16),
                   jax.ShapeDtypeStruct((t, LANES), F32)],
        compiler_params=_cparams("parallel"),
        name="ln1_router",
    )(v, ln_g, ln_b, mod, mod, w_router_pad)


def _rank_kernel(aff_ref, rank_ref, arow_ref, afft_scr, *, n):
    e = pl.program_id(1)

    @pl.when(e == 0)
    def _():
        afft_scr[...] = aff_ref[...].T

    a_row = afft_scr[pl.ds(e, 1), :]
    arow_ref[0, 0] = a_row
    rb = LANES
    t_row = lax.broadcasted_iota(jnp.int32, (rb, n), 1)
    lane = lax.broadcasted_iota(jnp.int32, (rb, LANES), 1)

    def body(c, cnt):
        r0 = pl.multiple_of(c * rb, rb)
        blk = aff_ref[pl.ds(r0, rb), :]
        a_col = jnp.sum(jnp.where(lane == e, blk, 0.0), axis=1, keepdims=True)
        t_col = r0 + lax.broadcasted_iota(jnp.int32, (rb, n), 0)
        beats = jnp.where(a_col > a_row, 1.0,
                          jnp.where(a_col == a_row, jnp.where(t_col < t_row, 1.0, 0.0), 0.0))
        return cnt + jnp.sum(beats, axis=0, keepdims=True)

    rank_ref[0, 0] = lax.fori_loop(0, n // rb, body, jnp.zeros((1, n), F32))


def _rank_call(aff, nsets, n, n_exp):
    out = jax.ShapeDtypeStruct((nsets, n_exp, 1, n), F32)
    spec = pl.BlockSpec((1, 1, 1, n), lambda s, e: (s, e, 0, 0))
    return pl.pallas_call(
        functools.partial(_rank_kernel, n=n),
        grid=(nsets, n_exp),
        in_specs=[pl.BlockSpec((n, LANES), lambda s, e: (s, 0))],
        out_specs=[spec, spec],
        out_shape=[out, out],
        scratch_shapes=[pltpu.VMEM((LANES, n), F32)],
        compiler_params=_cparams("parallel", "arbitrary"),
        name="ec_rank",
    )(aff)


def _gather_kernel(rank_ref, arow_ref, h_ref, hs_ref, gate_ref, *, cap):
    n = h_ref.shape[0]
    slot = lax.broadcasted_iota(jnp.int32, (cap, n), 0).astype(F32)
    sel = rank_ref[0, 0] == slot
    hs_ref[0] = _dot(jnp.where(sel, 1.0, 0.0).astype(BF16), h_ref[...]).astype(hs_ref.dtype)
    gate = jnp.sum(jnp.where(sel, arow_ref[0, 0], 0.0), axis=1, keepdims=True)
    gate_ref[0] = jnp.broadcast_to(gate, (cap, LANES))


def _gather_call(rank, arow, h2, nsets, n, cap, n_exp):
    d = h2.shape[1]
    rspec = pl.BlockSpec((1, 1, 1, n), lambda s, e: (s, e, 0, 0))
    return pl.pallas_call(
        functools.partial(_gather_kernel, cap=cap),
        grid=(nsets, n_exp),
        in_specs=[rspec, rspec, _resident((n, d), lambda s, e: (s, 0))],
        out_specs=[pl.BlockSpec((1, cap, d), lambda s, e: (e, s, 0)),
                   pl.BlockSpec((1, cap, LANES), lambda s, e: (e, s, 0))],
        out_shape=[jax.ShapeDtypeStruct((n_exp, nsets * cap, d), BF16),
                   jax.ShapeDtypeStruct((n_exp, nsets * cap, LANES), F32)],
        compiler_params=_cparams("parallel", "arbitrary"),
        name="ec_gather",
    )(rank, arow, h2)


def _ffn_up_kernel(xp_ref, xs_ref, w1_ref, w3_ref, o_ref, *, d_ff, tn):
    w1 = w1_ref[0].astype(BF16)
    w3 = w3_ref[0].astype(BF16)
    mp = xp_ref.shape[1]
    j = pl.program_id(1)
    for x_ref, rows in ((xp_ref, slice(0, mp)), (xs_ref, slice(mp, None))):
        x = x_ref[0]
        u = _dot(x, w1)
        v = _dot(x, w3)
        col = j * tn + lax.broadcasted_iota(jnp.int32, u.shape, 1)
        o_ref[0, rows, :] = jnp.where(col < d_ff, jax.nn.silu(u) * v, 0.0).astype(o_ref.dtype)


def _ffn_up_call(hsel_p, hsel_s, w1, w3, f_pad):
    n_exp, mp, d = hsel_p.shape
    ms = hsel_s.shape[1]
    d_ff = w1.shape[2]
    tn = MXU_DIM
    assert f_pad % tn == 0
    wspec = pl.BlockSpec((1, d, tn), lambda e, j: (e, 0, j))
    return pl.pallas_call(
        functools.partial(_ffn_up_kernel, d_ff=d_ff, tn=tn),
        grid=(n_exp, f_pad // tn),
        in_specs=[_resident((1, mp, d), lambda e, j: (e, 0, 0)),
                  _resident((1, ms, d), lambda e, j: (e, 0, 0)), wspec, wspec],
        out_specs=pl.BlockSpec((1, mp + ms, tn), lambda e, j: (e, 0, j)),
        out_shape=jax.ShapeDtypeStruct((n_exp, mp + ms, f_pad), BF16),
        compiler_params=_cparams("parallel", "arbitrary"),
        name="ffn_up",
    )(hsel_p, hsel_s, w1, w3)


def _ffn_down_kernel(a_ref, w_ref, gate_ref, hi_ref, lo_ref, acc_scr, *, kh, d_ff):
    k = pl.program_id(3)

    @pl.when(k == 0)
    def _():
        acc_scr[...] = _dot(a_ref[0, :, :kh], w_ref[0].astype(BF16))

    @pl.when(k == 1)
    def _():
        row = lax.broadcasted_iota(jnp.int32, w_ref.shape[1:], 0)
        w = jnp.where(row < d_ff - kh, w_ref[0], 0.0).astype(BF16)
        og = (acc_scr[...] + _dot(a_ref[0, :, kh:], w)) * gate_ref[0][:, :1]
        hi = og.astype(BF16)
        hi_ref[0] = hi
        lo_ref[0] = (og - hi.astype(F32)).astype(BF16)


def _ffn_down_call(a, w2, gate):
    n_exp, m, f_pad = a.shape
    d_ff, d = w2.shape[1:]
    kh = f_pad // 2
    assert kh % LANES == 0 and kh < d_ff <= f_pad
    tm = _pick(m, 1024, SUBLANES)
    tn = _pick(d, MXU_DIM, LANES)
    ospec = pl.BlockSpec((1, tm, tn), lambda e, i, j, k: (e, i, j))
    oshape = jax.ShapeDtypeStruct((n_exp, m, d), BF16)
    return pl.pallas_call(
        functools.partial(_ffn_down_kernel, kh=kh, d_ff=d_ff),
        grid=(n_exp, m // tm, d // tn, 2),
        in_specs=[_resident((1, tm, f_pad), lambda e, i, j, k: (e, i, 0)),
                  pl.BlockSpec((1, kh, tn), lambda e, i, j, k: (e, k, j)),
                  pl.BlockSpec((1, tm, LANES), lambda e, i, j, k: (e, i, 0))],
        out_specs=[ospec, ospec],
        out_shape=[oshape, oshape],
        scratch_shapes=[pltpu.VMEM((tm, tn), F32)],
        compiler_params=_cparams("parallel", "parallel", "arbitrary", "arbitrary"),
        name="ffn_down",
    )(a, w2, gate)


def _combine_kernel(rk_ref, hi_ref, lo_ref, x_ref, g2_ref, lg_ref, lb_ref, o_ref, acc_scr,
                    *, cap, ec, nk, base, rpm, tr, tiles_per_set, alpha):
    s = pl.program_id(0)
    i = pl.program_id(1)
    k = pl.program_id(2)
    kc = ec * cap
    d = x_ref.shape[1]
    rc = jnp.minimum(rk_ref[...], float(cap)).astype(BF16)
    src = lax.broadcasted_iota(jnp.int32, (LANES, kc), 0)
    col = lax.broadcasted_iota(jnp.int32, (LANES, kc), 1)
    expand = jnp.where(src == k * ec + col // cap, 1.0, 0.0).astype(BF16)
    r_exp = _dot(rc, expand)
    slot = (lax.broadcasted_iota(jnp.int32, (1, kc), 1) % cap).astype(F32)
    sel = jnp.where(r_exp == slot, 1.0, 0.0).astype(BF16)
    part = _dot(sel, hi_ref[...].reshape(kc, d)) + _dot(sel, lo_ref[...].reshape(kc, d))

    @pl.when(k == 0)
    def _():
        acc_scr[...] = part

    @pl.when(k > 0)
    def _():
        acc_scr[...] += part

    @pl.when(k == nk - 1)
    def _():
        r = _mod_row(s * tiles_per_set + i, tr, base, rpm)
        v = alpha * x_ref[...] + g2_ref[pl.ds(r, 1), :] * acc_scr[...]
        o_ref[...] = _layer_norm(v, lg_ref[...], lb_ref[...])


def _combine_call(rank_col, og_hi, og_lo, x1, mod, base, rpm, q_gate, ln_g, ln_b,
                  nsets, n, cap, slot_off, alpha):
    n_exp, _, d = og_hi.shape
    t = x1.shape[0]
    tr = _pick(n, 256, SUBLANES)
    tps = n // tr
    ec = max(1, min(n_exp, 512 // cap))
    nk = n_exp // ec
    assert slot_off % cap == 0 and n_exp % ec == 0 and rpm % tr == 0
    so = slot_off // cap
    ospec = pl.BlockSpec((ec, cap, d), lambda s, i, k: (k, so + s, 0))
    row = pl.BlockSpec((tr, d), lambda s, i, k: (s * tps + i, 0))
    vec = pl.BlockSpec((1, d), lambda s, i, k: (0, 0))
    return pl.pallas_call(
        functools.partial(_combine_kernel, cap=cap, ec=ec, nk=nk, base=base, rpm=rpm, tr=tr,
                          tiles_per_set=tps, alpha=alpha),
        grid=(nsets, tps, nk),
        in_specs=[pl.BlockSpec((tr, LANES), lambda s, i, k: (s * tps + i, 0)),
                  ospec, ospec, row,
                  pl.BlockSpec((MOD_ROWS, d), lambda s, i, k: (0, q_gate)), vec, vec],
        out_specs=row,
        out_shape=jax.ShapeDtypeStruct((t, d), F32),
        scratch_shapes=[pltpu.VMEM((tr, d), F32)],
        compiler_params=_cparams("parallel", "parallel", "arbitrary"),
        name="ec_combine",
    )(rank_col, og_hi, og_lo, x1, mod, ln_g, ln_b)


def _colmajor(x, nb, seq):
    d = x.shape[-1]
    return x.reshape(nb, seq // GRID_W, GRID_W, d).transpose(0, 2, 1, 3).reshape(nb * seq, d)


def _raster(x, nb, seq):
    d = x.shape[-1]
    return x.reshape(nb, GRID_W, seq // GRID_W, d).transpose(0, 2, 1, 3).reshape(nb * seq, d)


def _mixer_and_route(x, mod, base, rpm, p, states, is_latent, dims):
    nb, seq, d = x.shape
    t = nb * seq
    x2 = x.reshape(t, d)
    d_ssd, groups, hd, nstate, d_lru, n_exp, alpha = (dims[k] for k in
                                                      ('d_ssd', 'groups', 'hd', 'nstate', 'd_lru', 'n_exp', 'alpha'))
    d_xbc = d_ssd + 2 * groups * nstate
    heads = d_ssd // hd
    off_xbc = d_ssd
    off_dt = off_xbc + d_xbc
    off_gate = off_dt + 2 * heads
    off_lx = off_gate + d_lru
    off_merge = off_lx + d_lru

    h = _modulate_call(x2, mod, base, rpm, 0, 1)
    proj = _proj_in_call(h, p['w_in'])

    xbc = _conv_silu_call(proj, p['conv_xbc_w'], p['conv_xbc_b'], off_xbc, nb, seq)
    h0 = None if states is None else (states[0], states[1])
    y, s_ssd_f, s_ssd_b = _ssd_call(xbc, proj, off_dt, p['dt_bias'], p['a_log'], p['d_skip_x'],
                                    p['ssd_norm_w'], nb, seq, d_ssd, groups, hd, nstate, h0)

    if states is None:
        h0f = h0b = jnp.zeros((nb, 1, d_lru), F32)
    else:
        h0f, h0b = states[2].reshape(nb, 1, d_lru), states[3].reshape(nb, 1, d_lru)
    if is_latent:
        lx = _colmajor(lax.slice_in_dim(proj, off_lx, off_lx + d_lru, axis=1), nb, seq)
        hr, s_lru_f, s_lru_b = _lru_call(lx, 0, p, h0f, h0b, nb, seq)
        hr = _raster(hr, nb, seq)
    else:
        hr, s_lru_f, s_lru_b = _lru_call(proj, off_lx, p, h0f, h0b, nb, seq)
    hl = _gate_mul_call(proj, off_gate, hr)

    branch_ssd = _proj_ssd_call(y, p['w_proj_ssd'], proj, off_merge)
    merged = _proj_lru_call(hl, p['w_proj_lru'], proj, off_merge + d, branch_ssd)
    v1 = _proj_out_call(merged, p['w_out'], x2, mod, base, rpm, 2, alpha)
    x1, h2, aff = _ln_router_call(v1, p['ln1_g'], p['ln1_b'], mod, base, rpm, 3, 4, p['w_router_pad'], n_exp)
    rank, arow = _rank_call(aff, nb, seq, n_exp)
    return x1, h2, rank, arow, (s_ssd_f, s_ssd_b, s_lru_f, s_lru_b)


def _rank_columns(rank, nsets, n, n_exp):
    rc = rank.reshape(nsets, n_exp, n).transpose(0, 2, 1).reshape(nsets * n, n_exp)
    return jnp.pad(rc, ((0, 0), (0, LANES - n_exp)), constant_values=float(n))


def kernel(x_prompt, x_sample, state_ssd_fwd, state_ssd_bwd, state_lru_fwd, state_lru_bwd, c, c_ctx,
           w_mod, b_mod, w_in, conv_xbc_w, conv_xbc_b, dt_bias_fwd, dt_bias_bwd, a_log_fwd, a_log_bwd,
           d_skip, ssd_norm_w, w_proj_ssd, conv_lru_w, conv_lru_b,
           lru_wa_fwd, lru_ba_fwd, lru_wx_fwd, lru_bx_fwd, lru_lam_fwd,
           lru_wa_bwd, lru_ba_bwd, lru_wx_bwd, lru_bx_bwd, lru_lam_bwd,
           w_proj_lru, w_out, ln1_g, ln1_b, w_router, w1, w3, w2, ln2_g, ln2_b):
    depth, d, _ = w_mod.shape
    bp, seq_p, _ = x_prompt.shape
    bs, seq_s, _ = x_sample.shape
    heads, hd, nstate = state_ssd_fwd.shape[2:]
    d_ssd = w_proj_ssd.shape[1]
    d_xbc = conv_xbc_w.shape[2]
    groups = (d_xbc - d_ssd) // (2 * nstate)
    d_lru = w_proj_lru.shape[1]
    n_exp, _, d_ff = w1.shape[1:]
    alpha = (2.0 * depth) ** 0.25
    dims = dict(d_ssd=d_ssd, groups=groups, hd=hd, nstate=nstate, d_lru=d_lru, n_exp=n_exp, alpha=alpha)
    assert heads * hd == d_ssd and n_exp <= LANES and bs + 1 <= MOD_ROWS and seq_s % GRID_W == 0
    cap_p = EC_CAPACITY_FACTOR * seq_p // n_exp
    cap_s = EC_CAPACITY_FACTOR * seq_s // n_exp
    rows_p, rows_s = bp * cap_p, bs * cap_s
    f_pad = -(-d_ff // (2 * LANES)) * (2 * LANES)

    cv = jnp.zeros((MOD_ROWS, d), F32).at[0].set(c_ctx).at[1:1 + bs].set(c)

    y_p, y_s = x_prompt, x_sample
    new_states = ([], [], [], [])
    for layer in range(depth):
        p = dict(
            w_in=w_in[layer], conv_xbc_w=conv_xbc_w[layer], conv_xbc_b=conv_xbc_b[layer][None],
            dt_bias=jnp.stack([dt_bias_fwd[layer], dt_bias_bwd[layer]])[:, None, :],
            a_log=jnp.stack([a_log_fwd[layer], a_log_bwd[layer]])[:, None, :],
            d_skip_x=jnp.repeat(d_skip[layer], hd)[None], ssd_norm_w=ssd_norm_w[layer][None],
            w_proj_ssd=w_proj_ssd[layer], conv_lru_w=conv_lru_w[layer], conv_lru_b=conv_lru_b[layer][None],
            lru_wa_fwd=lru_wa_fwd[layer], lru_wx_fwd=lru_wx_fwd[layer],
            lru_ba_fwd=lru_ba_fwd[layer][None], lru_bx_fwd=lru_bx_fwd[layer][None],
            lru_lam_fwd=lru_lam_fwd[layer][None],
            lru_wa_bwd=lru_wa_bwd[layer], lru_wx_bwd=lru_wx_bwd[layer],
            lru_ba_bwd=lru_ba_bwd[layer][None], lru_bx_bwd=lru_bx_bwd[layer][None],
            lru_lam_bwd=lru_lam_bwd[layer][None],
            w_proj_lru=w_proj_lru[layer], w_out=w_out[layer],
            ln1_g=ln1_g[layer][None], ln1_b=ln1_b[layer][None],
            w_router_pad=jnp.pad(w_router[layer], ((0, 0), (0, LANES - n_exp))),
        )
        mod = _mod_call(cv, w_mod[layer], b_mod[layer][None])

        x1_p, h2_p, rank_p, arow_p, st = _mixer_and_route(
            y_p, mod, 0, bp * seq_p, p, None, False, dims)
        cached = (state_ssd_fwd[:, layer], state_ssd_bwd[:, layer], state_lru_fwd[:, layer], state_lru_bwd[:, layer])
        x1_s, h2_s, rank_s, arow_s, _ = _mixer_and_route(
            y_s, mod, 1, seq_s, p, cached, True, dims)
        for acc, s in zip(new_states, st):
            acc.append(s)

        hsel_p, gate_p = _gather_call(rank_p, arow_p, h2_p, bp, seq_p, cap_p, n_exp)
        hsel_s, gate_s = _gather_call(rank_s, arow_s, h2_s, bs, seq_s, cap_s, n_exp)
        act = _ffn_up_call(hsel_p, hsel_s, w1[layer], w3[layer], f_pad)
        og_hi, og_lo = _ffn_down_call(act, w2[layer], jnp.concatenate([gate_p, gate_s], axis=1))

        g2, b2 = ln2_g[layer][None], ln2_b[layer][None]
        y_p = _combine_call(_rank_columns(rank_p, bp, seq_p, n_exp), og_hi, og_lo, x1_p, mod, 0, bp * seq_p, 5,
                            g2, b2, bp, seq_p, cap_p, 0, alpha).reshape(bp, seq_p, d)
        y_s = _combine_call(_rank_columns(rank_s, bs, seq_s, n_exp), og_hi, og_lo, x1_s, mod, 1, seq_s, 5,
                            g2, b2, bs, seq_s, cap_s, rows_p, alpha).reshape(bs, seq_s, d)

    new_ssd_fwd = jnp.stack(new_states[0], axis=1)
    new_ssd_bwd = jnp.stack(new_states[1], axis=1)
    new_lru_fwd = jnp.concatenate(new_states[2], axis=1)
    new_lru_bwd = jnp.concatenate(new_states[3], axis=1)
    return (y_p, y_s, new_ssd_fwd, new_ssd_bwd, new_lru_fwd, new_lru_bwd)
```

```python
import functools

import jax
import jax.numpy as jnp
from jax import lax
from jax.experimental import pallas as pl
from jax.experimental.pallas import tpu as pltpu

F32 = jnp.float32
BF16 = jnp.bfloat16

LANES = 128
SUBLANES = 8
MXU_DIM = 256
VMEM_LIMIT = 56 * 1024 * 1024

GRID_W = 64
SSD_CHUNK = 128
LRU_C = 8.0
LN_EPS = 1e-5
EC_CAPACITY_FACTOR = 2
N_MOD = 6
MOD_ROWS = 8


def _pick(n, pref, align):
    best = None
    t = align
    while t <= min(n, pref):
        if n % t == 0:
            best = t
        t += align
    return best if best is not None else n


def _cparams(*sem):
    return pltpu.CompilerParams(dimension_semantics=sem, vmem_limit_bytes=VMEM_LIMIT)


def _resident(shape, index_map):
    return pl.BlockSpec(shape, index_map, pipeline_mode=pl.Buffered(1))


def _mod_row(i, tm, base, rows_per_mod):
    return base + (i * tm) // rows_per_mod


def _dot(a, b):
    return jnp.dot(a, b, preferred_element_type=F32)


def _mod_kernel(c_ref, w_ref, b_ref, o_ref):
    cs = jax.nn.silu(c_ref[...]).astype(BF16)
    o_ref[...] = _dot(cs, w_ref[...].astype(BF16)) + b_ref[...]


def _mod_call(cv, w_mod, b_mod):
    d, n = w_mod.shape
    tn = _pick(n, 512, LANES)
    return pl.pallas_call(
        _mod_kernel,
        grid=(n // tn,),
        in_specs=[pl.BlockSpec((MOD_ROWS, d), lambda j: (0, 0)),
                  pl.BlockSpec((d, tn), lambda j: (0, j)),
                  pl.BlockSpec((1, tn), lambda j: (0, j))],
        out_specs=pl.BlockSpec((MOD_ROWS, tn), lambda j: (0, j)),
        out_shape=jax.ShapeDtypeStruct((MOD_ROWS, n), F32),
        compiler_params=_cparams("parallel"),
        name="mod",
    )(cv, w_mod, b_mod)


def _modulate_kernel(x_ref, sh_ref, sc_ref, o_ref, *, tm, base, rpm):
    r = _mod_row(pl.program_id(0), tm, base, rpm)
    sc = sc_ref[pl.ds(r, 1), :]
    sh = sh_ref[pl.ds(r, 1), :]
    o_ref[...] = (x_ref[...] * (1.0 + sc) + sh).astype(o_ref.dtype)


def _modulate_call(x, mod, base, rpm, q_shift, q_scale):
    t, d = x.shape
    tm = _pick(t, 256, SUBLANES)
    assert rpm % tm == 0
    return pl.pallas_call(
        functools.partial(_modulate_kernel, tm=tm, base=base, rpm=rpm),
        grid=(t // tm,),
        in_specs=[pl.BlockSpec((tm, d), lambda i: (i, 0)),
                  pl.BlockSpec((MOD_ROWS, d), lambda i: (0, q_shift)),
                  pl.BlockSpec((MOD_ROWS, d), lambda i: (0, q_scale))],
        out_specs=pl.BlockSpec((tm, d), lambda i: (i, 0)),
        out_shape=jax.ShapeDtypeStruct((t, d), BF16),
        compiler_params=_cparams("parallel"),
        name="modulate",
    )(x, mod, mod)


def _proj_in_kernel(x_ref, w_ref, cw_ref, cb_ref, o_ref, *, j_lo, j_hi, seq):
    acc = _dot(x_ref[...], w_ref[...].astype(BF16))
    j = pl.program_id(1)
    is_xbc = jnp.logical_and(j >= j_lo, j < j_hi)

    @pl.when(is_xbc)
    def _():
        o_ref[...] = jax.nn.silu(_conv4(acc, cw_ref[...], cb_ref[...], seq))

    @pl.when(jnp.logical_not(is_xbc))
    def _():
        o_ref[...] = acc


def _proj_in_call(h, w, conv_w, conv_b, conv_off, seq):
    t, k = h.shape
    n = w.shape[1]
    tm = _pick(t, 2048, SUBLANES)
    tn = _pick(n, MXU_DIM, LANES)
    width = conv_w.shape[1]
    assert tm % seq == 0 and conv_off % tn == 0 and width % tn == 0 and conv_w.shape[0] == 4
    j_lo, nconv = conv_off // tn, width // tn

    def conv_blk(i, j):
        return (0, jnp.clip(j - j_lo, 0, nconv - 1))

    return pl.pallas_call(
        functools.partial(_proj_in_kernel, j_lo=j_lo, j_hi=j_lo + nconv, seq=seq),
        grid=(t // tm, n // tn),
        in_specs=[_resident((tm, k), lambda i, j: (i, 0)),
                  pl.BlockSpec((k, tn), lambda i, j: (0, j)),
                  pl.BlockSpec((4, tn), conv_blk),
                  pl.BlockSpec((1, tn), conv_blk)],
        out_specs=pl.BlockSpec((tm, tn), lambda i, j: (i, j)),
        out_shape=jax.ShapeDtypeStruct((t, n), F32),
        compiler_params=_cparams("parallel", "arbitrary"),
        name="proj_in",
    )(h, w, conv_w, conv_b)


def _proj_ssd_kernel(y_ref, w_ref, lg_ref, o_ref):
    acc = _dot(y_ref[...], w_ref[...].astype(BF16))
    o_ref[...] = jax.nn.sigmoid(lg_ref[...]) * acc


def _proj_ssd_call(y, w, proj, logit_off):
    t, k = y.shape
    n = w.shape[1]
    tm = _pick(t, 1024, SUBLANES)
    tn = _pick(n, MXU_DIM, LANES)
    assert logit_off % tn == 0
    lo = logit_off // tn
    return pl.pallas_call(
        _proj_ssd_kernel,
        grid=(t // tm, n // tn),
        in_specs=[_resident((tm, k), lambda i, j: (i, 0)),
                  pl.BlockSpec((k, tn), lambda i, j: (0, j)),
                  pl.BlockSpec((tm, tn), lambda i, j: (i, lo + j))],
        out_specs=pl.BlockSpec((tm, tn), lambda i, j: (i, j)),
        out_shape=jax.ShapeDtypeStruct((t, n), F32),
        compiler_params=_cparams("parallel", "arbitrary"),
        name="proj_ssd",
    )(y, w, proj)


def _proj_lru_kernel(h_ref, w_ref, lg_ref, s_ref, o_ref):
    acc = _dot(h_ref[...], w_ref[...].astype(BF16))
    o_ref[...] = (s_ref[...] + jax.nn.sigmoid(lg_ref[...]) * acc).astype(o_ref.dtype)


def _proj_lru_call(hl, w, proj, logit_off, branch_ssd):
    t, k = hl.shape
    n = w.shape[1]
    tm = _pick(t, 2048, SUBLANES)
    tn = _pick(n, MXU_DIM, LANES)
    assert logit_off % tn == 0
    lo = logit_off // tn
    return pl.pallas_call(
        _proj_lru_kernel,
        grid=(t // tm, n // tn),
        in_specs=[_resident((tm, k), lambda i, j: (i, 0)),
                  pl.BlockSpec((k, tn), lambda i, j: (0, j)),
                  pl.BlockSpec((tm, tn), lambda i, j: (i, lo + j)),
                  pl.BlockSpec((tm, tn), lambda i, j: (i, j))],
        out_specs=pl.BlockSpec((tm, tn), lambda i, j: (i, j)),
        out_shape=jax.ShapeDtypeStruct((t, n), BF16),
        compiler_params=_cparams("parallel", "arbitrary"),
        name="proj_lru",
    )(hl, w, proj, branch_ssd)


def _proj_out_kernel(m_ref, w_ref, x_ref, g_ref, o_ref, *, tm, base, rpm, alpha):
    r = _mod_row(pl.program_id(0), tm, base, rpm)
    acc = _dot(m_ref[...], w_ref[...].astype(BF16))
    o_ref[...] = alpha * x_ref[...] + g_ref[pl.ds(r, 1), :] * acc


def _proj_out_call(merged, w, x, mod, base, rpm, q_gate, alpha):
    t, k = merged.shape
    n = w.shape[1]
    tm = _pick(min(t, rpm), 2048, SUBLANES)
    tn = _pick(n, MXU_DIM, LANES)
    gq = q_gate * (n // tn)
    return pl.pallas_call(
        functools.partial(_proj_out_kernel, tm=tm, base=base, rpm=rpm, alpha=alpha),
        grid=(t // tm, n // tn),
        in_specs=[_resident((tm, k), lambda i, j: (i, 0)),
                  pl.BlockSpec((k, tn), lambda i, j: (0, j)),
                  pl.BlockSpec((tm, tn), lambda i, j: (i, j)),
                  pl.BlockSpec((MOD_ROWS, tn), lambda i, j: (0, gq + j))],
        out_specs=pl.BlockSpec((tm, tn), lambda i, j: (i, j)),
        out_shape=jax.ShapeDtypeStruct((t, n), F32),
        compiler_params=_cparams("parallel", "arbitrary"),
        name="proj_out",
    )(merged, w, x, mod)


def _shift_rows(x, d, reverse, fill, period=None):
    n = x.shape[0]
    if (period is None or period == n) and d % SUBLANES == 0:
        pad = jnp.full((d,) + x.shape[1:], fill, x.dtype)
        return jnp.concatenate([x[d:], pad] if reverse else [pad, x[:n - d]], axis=0)
    row = lax.broadcasted_iota(jnp.int32, x.shape, 0)
    if period is None or period == n:
        period = n
    else:
        row = row % period
    if reverse:
        return jnp.where(row < period - d, pltpu.roll(x, n - d, 0), fill)
    return jnp.where(row >= d, pltpu.roll(x, d, 0), fill)


def _cumsum_rows(a, reverse):
    d = 1
    while d < a.shape[0]:
        a = a + _shift_rows(a, d, reverse, 0.0)
        d *= 2
    return a


def _linear_scan(a, u, reverse):
    d = 1
    while d < a.shape[0]:
        u = a * _shift_rows(u, d, reverse, 0.0) + u
        if 2 * d < a.shape[0]:
            a = a * _shift_rows(a, d, reverse, 1.0)
        d *= 2
    return u


def _conv4(x, w, b, period=None):
    return (_shift_rows(x, 1, False, 0.0, period) * w[0:1, :] + x * w[1:2, :]
            + _shift_rows(x, 1, True, 0.0, period) * w[2:3, :]
            + _shift_rows(x, 2, True, 0.0, period) * w[3:4, :] + b)


def _ssd_chunk(reverse, g, xs_ref, b_ref, c_ref, dt_ref, dtb_ref, alog_ref, h_scr, *, hpg, hd, heads):
    q = SSD_CHUNK
    dt = jax.nn.softplus(dt_ref[...] + dtb_ref[0])
    a = dt * (-jnp.exp(alog_ref[0]))
    acum = _cumsum_rows(a, reverse)
    shift = jnp.where(g == 0, 0, heads - g * hpg)
    acum = pltpu.roll(acum, shift, 1)
    dt = pltpu.roll(dt, shift, 1)
    acum_t = acum.T
    cb16 = c_ref[...].astype(BF16)
    cb = lax.dot_general(cb16, b_ref[...].astype(BF16), (((1,), (1,)), ((), ())),
                         preferred_element_type=F32)
    bt16 = b_ref[...].T.astype(BF16)
    li = lax.broadcasted_iota(jnp.int32, (q, q), 0)
    si = lax.broadcasted_iota(jnp.int32, (q, q), 1)
    cbm = jnp.where((si >= li) if reverse else (si <= li), cb, 0.0)
    last = 0 if reverse else q - 1
    h_prev = h_scr[...]
    y_off = _dot(cb16, h_prev.astype(BF16))
    pw = 2 * hd
    lo = lax.broadcasted_iota(jnp.int32, (q, pw), 1) < hd
    ys = []
    for j in range(hpg // 2):
        k0, k1 = 2 * j, 2 * j + 1
        xp = xs_ref[:, j * pw:(j + 1) * pw]
        ac0 = jnp.broadcast_to(acum[:, k0:k0 + 1], (q, q))
        ac1 = jnp.broadcast_to(acum[:, k1:k1 + 1], (q, q))

        def scores(ac, k):
            return (cbm * jnp.exp(jnp.minimum(ac - acum_t[k:k + 1, :], 0.0))).astype(BF16)

        s2 = jnp.concatenate([scores(ac0, k0), scores(ac1, k1)], axis=1)
        xdt = xp * jnp.where(lo, dt[:, k0:k0 + 1], dt[:, k1:k1 + 1])
        rhs = jnp.concatenate([jnp.where(lo, xdt, 0.0), jnp.where(lo, 0.0, xdt)], axis=0).astype(BF16)
        y_diag = _dot(s2, rhs)
        a_pair = jnp.where(lo, ac0, ac1)
        e_pair = jnp.exp(a_pair)
        ys.append(y_diag + y_off[:, j * pw:(j + 1) * pw] * e_pair)
        w_pair = jnp.exp(a_pair[last:last + 1, :] - a_pair)
        st = _dot(bt16, (xdt * w_pair).astype(BF16))
        h_scr[:, j * pw:(j + 1) * pw] = h_prev[:, j * pw:(j + 1) * pw] * e_pair[last:last + 1, :] + st
    return jnp.concatenate(ys, axis=1)


def _ssd_load_state(h0_ref, h_scr, hpg, hd):
    pw = 2 * hd
    for j in range(hpg // 2):
        pair = jnp.concatenate([h0_ref[0, 2 * j], h0_ref[0, 2 * j + 1]], axis=0)
        h_scr[:, j * pw:(j + 1) * pw] = pair.T


def _ssd_store_state(s_ref, h_scr, hpg, hd):
    pw = 2 * hd
    for j in range(hpg // 2):
        pair = h_scr[:, j * pw:(j + 1) * pw].T
        s_ref[0, 2 * j] = pair[:hd]
        s_ref[0, 2 * j + 1] = pair[hd:]


def _ssd_kernel(*refs, nc, hpg, hd, heads, has_state):
    if has_state:
        (xs_ref, b_ref, c_ref, dt_ref, z_ref, dtb_ref, alog_ref, dsk_ref, nw_ref, h0f_ref, h0b_ref,
         y_ref, sf_ref, sb_ref, yf_scr, h_scr) = refs
    else:
        (xs_ref, b_ref, c_ref, dt_ref, z_ref, dtb_ref, alog_ref, dsk_ref, nw_ref,
         y_ref, sf_ref, sb_ref, yf_scr, h_scr) = refs
        h0f_ref = h0b_ref = None
    g = pl.program_id(1)
    step = pl.program_id(2)
    q = SSD_CHUNK
    chunk = functools.partial(_ssd_chunk, g=g, xs_ref=xs_ref, b_ref=b_ref, c_ref=c_ref, dt_ref=dt_ref,
                              dtb_ref=dtb_ref, alog_ref=alog_ref, h_scr=h_scr, hpg=hpg, hd=hd, heads=heads)

    def init(h0_ref):
        if has_state:
            _ssd_load_state(h0_ref, h_scr, hpg, hd)
        else:
            h_scr[...] = jnp.zeros_like(h_scr)

    @pl.when(step == 0)
    def _():
        init(h0f_ref)

    @pl.when(step == nc)
    def _():
        init(h0b_ref)

    @pl.when(step < nc)
    def _():
        row0 = pl.multiple_of(step * q, q)
        yf_scr[pl.ds(row0, q), :] = chunk(False)

    @pl.when(step == nc - 1)
    def _():
        _ssd_store_state(sf_ref, h_scr, hpg, hd)

    @pl.when(step >= nc)
    def _():
        row0 = pl.multiple_of((2 * nc - 1 - step) * q, q)
        y = yf_scr[pl.ds(row0, q), :] + chunk(True) + dsk_ref[...] * xs_ref[...]
        y = y * jax.nn.silu(z_ref[...])
        y = y * lax.rsqrt(jnp.mean(jnp.square(y), axis=-1, keepdims=True) + LN_EPS)
        y_ref[...] = (y * nw_ref[...]).astype(y_ref.dtype)

    @pl.when(step == 2 * nc - 1)
    def _():
        _ssd_store_state(sb_ref, h_scr, hpg, hd)


def _ssd_call(proj, xbc_off, dt_off, dt_bias, a_log, d_skip_x, norm_w, nb, seq, d_ssd, groups, hd, nstate, h0):
    q = SSD_CHUNK
    nc = seq // q
    gw = d_ssd // groups
    heads = d_ssd // hd
    hpg = heads // groups
    assert seq % q == 0 and hpg % 2 == 0 and 2 * hd == LANES and nstate == LANES and heads == LANES
    assert dt_off % heads == 0 and xbc_off % gw == 0 and xbc_off % nstate == 0
    dto = dt_off // heads
    xo = xbc_off // gw
    bo = (xbc_off + d_ssd) // nstate

    def chunk_of(s):
        return jnp.where(s < nc, s, 2 * nc - 1 - s)

    def out_chunk(s):
        return jnp.where(s < nc, nc - 1, 2 * nc - 1 - s)

    in_specs = [
        pl.BlockSpec((q, gw), lambda b, g, s: (b * nc + chunk_of(s), xo + g)),
        pl.BlockSpec((q, nstate), lambda b, g, s: (b * nc + chunk_of(s), bo + g)),
        pl.BlockSpec((q, nstate), lambda b, g, s: (b * nc + chunk_of(s), bo + groups + g)),
        pl.BlockSpec((q, heads), lambda b, g, s: (b * nc + chunk_of(s), dto + s // nc)),
        pl.BlockSpec((q, gw), lambda b, g, s: (b * nc + out_chunk(s), g)),
        pl.BlockSpec((1, 1, heads), lambda b, g, s: (s // nc, 0, 0)),
        pl.BlockSpec((1, 1, heads), lambda b, g, s: (s // nc, 0, 0)),
        pl.BlockSpec((1, gw), lambda b, g, s: (0, g)),
        pl.BlockSpec((1, gw), lambda b, g, s: (0, g)),
    ]
    args = [proj, proj, proj, proj, proj, dt_bias, a_log, d_skip_x, norm_w]
    state_spec = pl.BlockSpec((1, hpg, hd, nstate), lambda b, g, s: (b, g, 0, 0))
    if h0 is not None:
        in_specs += [state_spec, state_spec]
        args += [h0[0], h0[1]]
    state_shape = jax.ShapeDtypeStruct((nb, heads, hd, nstate), F32)
    return pl.pallas_call(
        functools.partial(_ssd_kernel, nc=nc, hpg=hpg, hd=hd, heads=heads, has_state=h0 is not None),
        grid=(nb, groups, 2 * nc),
        in_specs=in_specs,
        out_specs=[pl.BlockSpec((q, gw), lambda b, g, s: (b * nc + out_chunk(s), g)), state_spec, state_spec],
        out_shape=[jax.ShapeDtypeStruct((nb * seq, d_ssd), BF16), state_shape, state_shape],
        scratch_shapes=[pltpu.VMEM((seq, gw), F32), pltpu.VMEM((nstate, gw), F32)],
        compiler_params=_cparams("parallel", "parallel", "arbitrary"),
        name="ssd",
    )(*args)


def _lru_dir(x, xb, wa_ref, wx_ref, ba_ref, bx_ref, lam_ref, h0, reverse):
    r = jax.nn.sigmoid(_dot(xb, wa_ref[0].astype(BF16)) + ba_ref[...])
    i = jax.nn.sigmoid(_dot(xb, wx_ref[0].astype(BF16)) + bx_ref[...])
    log_a = -LRU_C * jax.nn.softplus(-lam_ref[...]) * r
    a = jnp.exp(log_a)
    u = jnp.sqrt(-jnp.tanh(log_a) * (a * a + 1.0)) * (i * x)
    first = x.shape[0] - 1 if reverse else 0
    row = lax.broadcasted_iota(jnp.int32, x.shape, 0)
    u = u + jnp.where(row == first, a * h0, 0.0)
    return _linear_scan(a, u, reverse)


def _lru_kernel(x_ref, cw_ref, cb_ref, waf_ref, wxf_ref, baf_ref, bxf_ref, lamf_ref,
                wab_ref, wxb_ref, bab_ref, bxb_ref, lamb_ref, h0f_ref, h0b_ref,
                hr_ref, sf_ref, sb_ref):
    x = _conv4(x_ref[...], cw_ref[...], cb_ref[...])
    xb = x.astype(BF16)
    n = x.shape[0]
    h_f = _lru_dir(x, xb, waf_ref, wxf_ref, baf_ref, bxf_ref, lamf_ref, h0f_ref[0], False)
    h_b = _lru_dir(x, xb, wab_ref, wxb_ref, bab_ref, bxb_ref, lamb_ref, h0b_ref[0], True)
    hr_ref[...] = h_f + h_b
    sf_ref[0] = h_f[n - 1:n, :]
    sb_ref[0] = h_b[0:1, :]


def _lru_call(xsrc, col_off, p, h0f, h0b, nb, seq):
    blocks, bw, _ = p['lru_wa_fwd'].shape
    d = blocks * bw
    assert col_off % bw == 0
    co = col_off // bw
    vec = pl.BlockSpec((1, bw), lambda b, j: (0, j))
    wblk = pl.BlockSpec((1, bw, bw), lambda b, j: (j, 0, 0))
    st = pl.BlockSpec((1, 1, bw), lambda b, j: (b, 0, j))
    st_shape = jax.ShapeDtypeStruct((nb, 1, d), F32)
    return pl.pallas_call(
        _lru_kernel,
        grid=(nb, blocks),
        in_specs=[pl.BlockSpec((seq, bw), lambda b, j: (b, co + j)),
                  pl.BlockSpec((4, bw), lambda b, j: (0, j)), vec,
                  wblk, wblk, vec, vec, vec, wblk, wblk, vec, vec, vec, st, st],
        out_specs=[pl.BlockSpec((seq, bw), lambda b, j: (b, j)), st, st],
        out_shape=[jax.ShapeDtypeStruct((nb * seq, d), F32), st_shape, st_shape],
        compiler_params=_cparams("parallel", "parallel"),
        name="rglru",
    )(xsrc, p['conv_lru_w'], p['conv_lru_b'],
      p['lru_wa_fwd'], p['lru_wx_fwd'], p['lru_ba_fwd'], p['lru_bx_fwd'], p['lru_lam_fwd'],
      p['lru_wa_bwd'], p['lru_wx_bwd'], p['lru_ba_bwd'], p['lru_bx_bwd'], p['lru_lam_bwd'],
      h0f, h0b)


def _gate_mul_kernel(g_ref, h_ref, o_ref):
    o_ref[...] = (jax.nn.gelu(g_ref[...]) * h_ref[...]).astype(o_ref.dtype)


def _gate_mul_call(proj, gate_off, hr):
    t, d = hr.shape
    tm = _pick(t, 1024, SUBLANES)
    tn = _pick(d, MXU_DIM, LANES)
    assert gate_off % tn == 0
    go = gate_off // tn
    return pl.pallas_call(
        _gate_mul_kernel,
        grid=(t // tm, d // tn),
        in_specs=[pl.BlockSpec((tm, tn), lambda i, j: (i, go + j)),
                  pl.BlockSpec((tm, tn), lambda i, j: (i, j))],
        out_specs=pl.BlockSpec((tm, tn), lambda i, j: (i, j)),
        out_shape=jax.ShapeDtypeStruct((t, d), BF16),
        compiler_params=_cparams("parallel", "parallel"),
        name="lru_gate",
    )(proj, hr)


def _layer_norm(v, g, b):
    mu = jnp.mean(v, axis=-1, keepdims=True)
    var = jnp.mean(jnp.square(v - mu), axis=-1, keepdims=True)
    return (v - mu) * lax.rsqrt(var + LN_EPS) * g + b


def _ln_router_kernel(v_ref, g_ref, b_ref, sh_ref, sc_ref, wr_ref, x1_ref, h2_ref, aff_ref,
                      *, tm, base, rpm, n_exp):
    r = _mod_row(pl.program_id(0), tm, base, rpm)
    x1 = _layer_norm(v_ref[...], g_ref[...], b_ref[...])
    x1_ref[...] = x1
    h2 = (x1 * (1.0 + sc_ref[pl.ds(r, 1), :]) + sh_ref[pl.ds(r, 1), :]).astype(BF16)
    h2_ref[...] = h2
    logits = _dot(h2, wr_ref[...].astype(BF16))
    lane = lax.broadcasted_iota(jnp.int32, logits.shape, 1)
    logits = jnp.where(lane < n_exp, logits, -jnp.inf)
    e = jnp.exp(logits - jnp.max(logits, axis=-1, keepdims=True))
    aff_ref[...] = e / jnp.sum(e, axis=-1, keepdims=True)


def _ln_router_call(v, ln_g, ln_b, mod, base, rpm, q_shift, q_scale, w_router_pad, n_exp):
    t, d = v.shape
    tm = _pick(t, 256, SUBLANES)
    assert rpm % tm == 0
    row = pl.BlockSpec((tm, d), lambda i: (i, 0))
    vec = pl.BlockSpec((1, d), lambda i: (0, 0))
    return pl.pallas_call(
        functools.partial(_ln_router_kernel, tm=tm, base=base, rpm=rpm, n_exp=n_exp),
        grid=(t // tm,),
        in_specs=[row, vec, vec,
                  pl.BlockSpec((MOD_ROWS, d), lambda i: (0, q_shift)),
                  pl.BlockSpec((MOD_ROWS, d), lambda i: (0, q_scale)),
                  pl.BlockSpec((d, LANES), lambda i: (0, 0))],
        out_specs=[row, row, pl.BlockSpec((tm, LANES), lambda i: (i, 0))],
        out_shape=[jax.ShapeDtypeStruct((t, d), F32), jax.ShapeDtypeStruct((t, d), BF16),
                   jax.ShapeDtypeStruct((t, LANES), F32)],
        compiler_params=_cparams("parallel"),
        name="ln1_router",
    )(v, ln_g, ln_b, mod, mod, w_router_pad)


def _experts_per_step(n, n_exp):
    return n_exp if n <= 512 else 1


def _rank_kernel(aff_ref, rank_ref, arow_ref, afft_scr, *, n, eps):
    eb = pl.program_id(1)

    @pl.when(eb == 0)
    def _():
        afft_scr[...] = aff_ref[...].T

    rb = LANES
    t_row = lax.broadcasted_iota(jnp.int32, (rb, n), 1)
    lane = lax.broadcasted_iota(jnp.int32, (rb, LANES), 1)
    for idx in range(eps):
        e = eb * eps + idx
        a_row = afft_scr[pl.ds(e, 1), :]
        arow_ref[0, idx] = a_row

        def body(c, cnt, e=e, a_row=a_row):
            r0 = pl.multiple_of(c * rb, rb)
            blk = aff_ref[pl.ds(r0, rb), :]
            a_col = jnp.sum(jnp.where(lane == e, blk, 0.0), axis=1, keepdims=True)
            t_col = r0 + lax.broadcasted_iota(jnp.int32, (rb, n), 0)
            beats = jnp.where(a_col > a_row, 1.0,
                              jnp.where(a_col == a_row, jnp.where(t_col < t_row, 1.0, 0.0), 0.0))
            return cnt + jnp.sum(beats, axis=0, keepdims=True)

        rank_ref[0, idx] = lax.fori_loop(0, n // rb, body, jnp.zeros((1, n), F32))


def _rank_call(aff, nsets, n, n_exp):
    eps = _experts_per_step(n, n_exp)
    out = jax.ShapeDtypeStruct((nsets, n_exp, 1, n), F32)
    spec = pl.BlockSpec((1, eps, 1, n), lambda s, e: (s, e, 0, 0))
    return pl.pallas_call(
        functools.partial(_rank_kernel, n=n, eps=eps),
        grid=(nsets, n_exp // eps),
        in_specs=[pl.BlockSpec((n, LANES), lambda s, e: (s, 0))],
        out_specs=[spec, spec],
        out_shape=[out, out],
        scratch_shapes=[pltpu.VMEM((LANES, n), F32)],
        compiler_params=_cparams("parallel", "arbitrary"),
        name="ec_rank",
    )(aff)


def _gather_kernel(rank_ref, arow_ref, h_ref, hs_ref, gate_ref, *, cap, eps):
    n, d = h_ref.shape
    slot = lax.broadcasted_iota(jnp.int32, (cap, n), 0).astype(F32)
    sels = [rank_ref[0, idx] == slot for idx in range(eps)]
    onehot = jnp.concatenate([jnp.where(s, 1.0, 0.0) for s in sels], axis=0).astype(BF16)
    hs_ref[...] = _dot(onehot, h_ref[...]).reshape(eps, cap, d).astype(hs_ref.dtype)
    for idx in range(eps):
        gate = jnp.sum(jnp.where(sels[idx], arow_ref[0, idx], 0.0), axis=1, keepdims=True)
        gate_ref[idx] = jnp.broadcast_to(gate, (cap, LANES))


def _gather_call(rank, arow, h2, nsets, n, cap, n_exp):
    d = h2.shape[1]
    eps = _experts_per_step(n, n_exp)
    rspec = pl.BlockSpec((1, eps, 1, n), lambda s, e: (s, e, 0, 0))
    return pl.pallas_call(
        functools.partial(_gather_kernel, cap=cap, eps=eps),
        grid=(nsets, n_exp // eps),
        in_specs=[rspec, rspec, _resident((n, d), lambda s, e: (s, 0))],
        out_specs=[pl.BlockSpec((eps, cap, d), lambda s, e: (e, s, 0)),
                   pl.BlockSpec((eps, cap, LANES), lambda s, e: (e, s, 0))],
        out_shape=[jax.ShapeDtypeStruct((n_exp, nsets * cap, d), BF16),
                   jax.ShapeDtypeStruct((n_exp, nsets * cap, LANES), F32)],
        compiler_params=_cparams("parallel", "arbitrary"),
        name="ec_gather",
    )(rank, arow, h2)


def _dot_nt(a, b):
    return lax.dot_general(a, b, (((1,), (1,)), ((), ())), preferred_element_type=F32)


def _ffn_up_kernel(xp_ref, xs_ref, w1_ref, w3_ref, o_ref, *, d_ff, tn):
    w1 = w1_ref[0].astype(BF16)
    w3 = w3_ref[0].astype(BF16)
    mp = xp_ref.shape[1]
    j = pl.program_id(1)
    for x_ref, rows in ((xp_ref, slice(0, mp)), (xs_ref, slice(mp, None))):
        x = x_ref[0]
        u = _dot_nt(x, w1)
        v = _dot_nt(x, w3)
        col = j * tn + lax.broadcasted_iota(jnp.int32, u.shape, 1)
        o_ref[0, rows, :] = jnp.where(col < d_ff, jax.nn.silu(u) * v, 0.0).astype(o_ref.dtype)


def _ffn_up_call(hsel_p, hsel_s, w1t, w3t, f_pad):
    n_exp, mp, d = hsel_p.shape
    ms = hsel_s.shape[1]
    d_ff = w1t.shape[1]
    tn = MXU_DIM
    assert f_pad % tn == 0
    wspec = pl.BlockSpec((1, tn, d), lambda e, j: (e, j, 0))
    return pl.pallas_call(
        functools.partial(_ffn_up_kernel, d_ff=d_ff, tn=tn),
        grid=(n_exp, f_pad // tn),
        in_specs=[_resident((1, mp, d), lambda e, j: (e, 0, 0)),
                  _resident((1, ms, d), lambda e, j: (e, 0, 0)), wspec, wspec],
        out_specs=pl.BlockSpec((1, mp + ms, tn), lambda e, j: (e, 0, j)),
        out_shape=jax.ShapeDtypeStruct((n_exp, mp + ms, f_pad), BF16),
        compiler_params=_cparams("parallel", "arbitrary"),
        name="ffn_up",
    )(hsel_p, hsel_s, w1t, w3t)


def _ffn_down_kernel(a_ref, w_ref, gate_ref, hi_ref, lo_ref, acc_scr, *, kh, d_ff):
    k = pl.program_id(3)

    @pl.when(k == 0)
    def _():
        acc_scr[...] = _dot(a_ref[0, :, :kh], w_ref[0].astype(BF16))

    @pl.when(k == 1)
    def _():
        row = lax.broadcasted_iota(jnp.int32, w_ref.shape[1:], 0)
        w = jnp.where(row < d_ff - kh, w_ref[0], 0.0).astype(BF16)
        og = (acc_scr[...] + _dot(a_ref[0, :, kh:], w)) * gate_ref[0][:, :1]
        hi = og.astype(BF16)
        hi_ref[0] = hi
        lo_ref[0] = (og - hi.astype(F32)).astype(BF16)


def _ffn_down_call(a, w2, gate):
    n_exp, m, f_pad = a.shape
    d_ff, d = w2.shape[1:]
    kh = f_pad // 2
    assert kh % LANES == 0 and kh < d_ff <= f_pad
    tm = _pick(m, 1024, SUBLANES)
    tn = _pick(d, MXU_DIM, LANES)
    ospec = pl.BlockSpec((1, tm, tn), lambda e, i, j, k: (e, i, j))
    oshape = jax.ShapeDtypeStruct((n_exp, m, d), BF16)
    return pl.pallas_call(
        functools.partial(_ffn_down_kernel, kh=kh, d_ff=d_ff),
        grid=(n_exp, m // tm, d // tn, 2),
        in_specs=[_resident((1, tm, f_pad), lambda e, i, j, k: (e, i, 0)),
                  pl.BlockSpec((1, kh, tn), lambda e, i, j, k: (e, k, j)),
                  pl.BlockSpec((1, tm, LANES), lambda e, i, j, k: (e, i, 0))],
        out_specs=[ospec, ospec],
        out_shape=[oshape, oshape],
        scratch_shapes=[pltpu.VMEM((tm, tn), F32)],
        compiler_params=_cparams("parallel", "parallel", "arbitrary", "arbitrary"),
        name="ffn_down",
    )(a, w2, gate)


def _combine_kernel(rk_ref, hi_ref, lo_ref, x_ref, g2_ref, lg_ref, lb_ref, o_ref, acc_scr,
                    *, cap, ec, nk, base, rpm, tr, tiles_per_set, alpha):
    s = pl.program_id(0)
    i = pl.program_id(1)
    k = pl.program_id(2)
    kc = ec * cap
    d = x_ref.shape[1]
    rc = jnp.minimum(rk_ref[...], float(cap)).astype(BF16)
    src = lax.broadcasted_iota(jnp.int32, (LANES, kc), 0)
    col = lax.broadcasted_iota(jnp.int32, (LANES, kc), 1)
    expand = jnp.where(src == k * ec + col // cap, 1.0, 0.0).astype(BF16)
    r_exp = _dot(rc, expand)
    slot = (lax.broadcasted_iota(jnp.int32, (1, kc), 1) % cap).astype(F32)
    sel = jnp.where(r_exp == slot, 1.0, 0.0).astype(BF16)
    part = _dot(sel, hi_ref[...].reshape(kc, d)) + _dot(sel, lo_ref[...].reshape(kc, d))

    @pl.when(k == 0)
    def _():
        acc_scr[...] = part

    @pl.when(k > 0)
    def _():
        acc_scr[...] += part

    @pl.when(k == nk - 1)
    def _():
        r = _mod_row(s * tiles_per_set + i, tr, base, rpm)
        v = alpha * x_ref[...] + g2_ref[pl.ds(r, 1), :] * acc_scr[...]
        o_ref[...] = _layer_norm(v, lg_ref[...], lb_ref[...])


def _combine_call(rank_col, og_hi, og_lo, x1, mod, base, rpm, q_gate, ln_g, ln_b,
                  nsets, n, cap, slot_off, alpha):
    n_exp, _, d = og_hi.shape
    t = x1.shape[0]
    tr = _pick(n, 256, SUBLANES)
    tps = n // tr
    ec = max(1, min(n_exp, 512 // cap))
    nk = n_exp // ec
    assert slot_off % cap == 0 and n_exp % ec == 0 and rpm % tr == 0
    so = slot_off // cap
    ospec = pl.BlockSpec((ec, cap, d), lambda s, i, k: (k, so + s, 0))
    row = pl.BlockSpec((tr, d), lambda s, i, k: (s * tps + i, 0))
    vec = pl.BlockSpec((1, d), lambda s, i, k: (0, 0))
    return pl.pallas_call(
        functools.partial(_combine_kernel, cap=cap, ec=ec, nk=nk, base=base, rpm=rpm, tr=tr,
                          tiles_per_set=tps, alpha=alpha),
        grid=(nsets, tps, nk),
        in_specs=[pl.BlockSpec((tr, LANES), lambda s, i, k: (s * tps + i, 0)),
                  ospec, ospec, row,
                  pl.BlockSpec((MOD_ROWS, d), lambda s, i, k: (0, q_gate)), vec, vec],
        out_specs=row,
        out_shape=jax.ShapeDtypeStruct((t, d), F32),
        scratch_shapes=[pltpu.VMEM((tr, d), F32)],
        compiler_params=_cparams("parallel", "parallel", "arbitrary"),
        name="ec_combine",
    )(rank_col, og_hi, og_lo, x1, mod, ln_g, ln_b)


def _colmajor(x, nb, seq):
    d = x.shape[-1]
    return x.reshape(nb, seq // GRID_W, GRID_W, d).transpose(0, 2, 1, 3).reshape(nb * seq, d)


def _raster(x, nb, seq):
    d = x.shape[-1]
    return x.reshape(nb, GRID_W, seq // GRID_W, d).transpose(0, 2, 1, 3).reshape(nb * seq, d)


def _mixer_and_route(x, mod, base, rpm, p, states, is_latent, dims):
    nb, seq, d = x.shape
    t = nb * seq
    x2 = x.reshape(t, d)
    d_ssd, groups, hd, nstate, d_lru, n_exp, alpha = (dims[k] for k in
                                                      ('d_ssd', 'groups', 'hd', 'nstate', 'd_lru', 'n_exp', 'alpha'))
    d_xbc = d_ssd + 2 * groups * nstate
    heads = d_ssd // hd
    off_xbc = d_ssd
    off_dt = off_xbc + d_xbc
    off_gate = off_dt + 2 * heads
    off_lx = off_gate + d_lru
    off_merge = off_lx + d_lru

    h = _modulate_call(x2, mod, base, rpm, 0, 1)
    proj = _proj_in_call(h, p['w_in'], p['conv_xbc_w'], p['conv_xbc_b'], off_xbc, seq)

    h0 = None if states is None else (states[0], states[1])
    y, s_ssd_f, s_ssd_b = _ssd_call(proj, off_xbc, off_dt, p['dt_bias'], p['a_log'], p['d_skip_x'],
                                    p['ssd_norm_w'], nb, seq, d_ssd, groups, hd, nstate, h0)

    if states is None:
        h0f = h0b = jnp.zeros((nb, 1, d_lru), F32)
    else:
        h0f, h0b = states[2].reshape(nb, 1, d_lru), states[3].reshape(nb, 1, d_lru)
    if is_latent:
        lx = _colmajor(lax.slice_in_dim(proj, off_lx, off_lx + d_lru, axis=1), nb, seq)
        hr, s_lru_f, s_lru_b = _lru_call(lx, 0, p, h0f, h0b, nb, seq)
        hr = _raster(hr, nb, seq)
    else:
        hr, s_lru_f, s_lru_b = _lru_call(proj, off_lx, p, h0f, h0b, nb, seq)
    hl = _gate_mul_call(proj, off_gate, hr)

    branch_ssd = _proj_ssd_call(y, p['w_proj_ssd'], proj, off_merge)
    merged = _proj_lru_call(hl, p['w_proj_lru'], proj, off_merge + d, branch_ssd)
    v1 = _proj_out_call(merged, p['w_out'], x2, mod, base, rpm, 2, alpha)
    x1, h2, aff = _ln_router_call(v1, p['ln1_g'], p['ln1_b'], mod, base, rpm, 3, 4, p['w_router_pad'], n_exp)
    rank, arow = _rank_call(aff, nb, seq, n_exp)
    return x1, h2, rank, arow, (s_ssd_f, s_ssd_b, s_lru_f, s_lru_b)


def _rank_columns(rank, nsets, n, n_exp):
    rc = rank.reshape(nsets, n_exp, n).transpose(0, 2, 1).reshape(nsets * n, n_exp)
    return jnp.pad(rc, ((0, 0), (0, LANES - n_exp)), constant_values=float(n))


def kernel(x_prompt, x_sample, state_ssd_fwd, state_ssd_bwd, state_lru_fwd, state_lru_bwd, c, c_ctx,
           w_mod, b_mod, w_in, conv_xbc_w, conv_xbc_b, dt_bias_fwd, dt_bias_bwd, a_log_fwd, a_log_bwd,
           d_skip, ssd_norm_w, w_proj_ssd, conv_lru_w, conv_lru_b,
           lru_wa_fwd, lru_ba_fwd, lru_wx_fwd, lru_bx_fwd, lru_lam_fwd,
           lru_wa_bwd, lru_ba_bwd, lru_wx_bwd, lru_bx_bwd, lru_lam_bwd,
           w_proj_lru, w_out, ln1_g, ln1_b, w_router, w1, w3, w2, ln2_g, ln2_b):
    depth, d, _ = w_mod.shape
    bp, seq_p, _ = x_prompt.shape
    bs, seq_s, _ = x_sample.shape
    heads, hd, nstate = state_ssd_fwd.shape[2:]
    d_ssd = w_proj_ssd.shape[1]
    d_xbc = conv_xbc_w.shape[2]
    groups = (d_xbc - d_ssd) // (2 * nstate)
    d_lru = w_proj_lru.shape[1]
    n_exp, _, d_ff = w1.shape[1:]
    alpha = (2.0 * depth) ** 0.25
    dims = dict(d_ssd=d_ssd, groups=groups, hd=hd, nstate=nstate, d_lru=d_lru, n_exp=n_exp, alpha=alpha)
    assert heads * hd == d_ssd and n_exp <= LANES and bs + 1 <= MOD_ROWS and seq_s % GRID_W == 0
    cap_p = EC_CAPACITY_FACTOR * seq_p // n_exp
    cap_s = EC_CAPACITY_FACTOR * seq_s // n_exp
    rows_p, rows_s = bp * cap_p, bs * cap_s
    f_pad = -(-d_ff // (2 * LANES)) * (2 * LANES)

    cv = jnp.zeros((MOD_ROWS, d), F32).at[0].set(c_ctx).at[1:1 + bs].set(c)

    y_p, y_s = x_prompt, x_sample
    new_states = ([], [], [], [])
    for layer in range(depth):
        p = dict(
            w_in=w_in[layer], conv_xbc_w=conv_xbc_w[layer], conv_xbc_b=conv_xbc_b[layer][None],
            dt_bias=jnp.stack([dt_bias_fwd[layer], dt_bias_bwd[layer]])[:, None, :],
            a_log=jnp.stack([a_log_fwd[layer], a_log_bwd[layer]])[:, None, :],
            d_skip_x=jnp.repeat(d_skip[layer], hd)[None], ssd_norm_w=ssd_norm_w[layer][None],
            w_proj_ssd=w_proj_ssd[layer], conv_lru_w=conv_lru_w[layer], conv_lru_b=conv_lru_b[layer][None],
            lru_wa_fwd=lru_wa_fwd[layer], lru_wx_fwd=lru_wx_fwd[layer],
            lru_ba_fwd=lru_ba_fwd[layer][None], lru_bx_fwd=lru_bx_fwd[layer][None],
            lru_lam_fwd=lru_lam_fwd[layer][None],
            lru_wa_bwd=lru_wa_bwd[layer], lru_wx_bwd=lru_wx_bwd[layer],
            lru_ba_bwd=lru_ba_bwd[layer][None], lru_bx_bwd=lru_bx_bwd[layer][None],
            lru_lam_bwd=lru_lam_bwd[layer][None],
            w_proj_lru=w_proj_lru[layer], w_out=w_out[layer],
            ln1_g=ln1_g[layer][None], ln1_b=ln1_b[layer][None],
            w_router_pad=jnp.pad(w_router[layer], ((0, 0), (0, LANES - n_exp))),
        )
        mod = _mod_call(cv, w_mod[layer], b_mod[layer][None])

        x1_p, h2_p, rank_p, arow_p, st = _mixer_and_route(
            y_p, mod, 0, bp * seq_p, p, None, False, dims)
        cached = (state_ssd_fwd[:, layer], state_ssd_bwd[:, layer], state_lru_fwd[:, layer], state_lru_bwd[:, layer])
        x1_s, h2_s, rank_s, arow_s, _ = _mixer_and_route(
            y_s, mod, 1, seq_s, p, cached, True, dims)
        for acc, s in zip(new_states, st):
            acc.append(s)

        hsel_p, gate_p = _gather_call(rank_p, arow_p, h2_p, bp, seq_p, cap_p, n_exp)
        hsel_s, gate_s = _gather_call(rank_s, arow_s, h2_s, bs, seq_s, cap_s, n_exp)
        act = _ffn_up_call(hsel_p, hsel_s, jnp.swapaxes(w1[layer], 1, 2), jnp.swapaxes(w3[layer], 1, 2), f_pad)
        og_hi, og_lo = _ffn_down_call(act, w2[layer], jnp.concatenate([gate_p, gate_s], axis=1))

        g2, b2 = ln2_g[layer][None], ln2_b[layer][None]
        y_p = _combine_call(_rank_columns(rank_p, bp, seq_p, n_exp), og_hi, og_lo, x1_p, mod, 0, bp * seq_p, 5,
                            g2, b2, bp, seq_p, cap_p, 0, alpha).reshape(bp, seq_p, d)
        y_s = _combine_call(_rank_columns(rank_s, bs, seq_s, n_exp), og_hi, og_lo, x1_s, mod, 1, seq_s, 5,
                            g2, b2, bs, seq_s, cap_s, rows_p, alpha).reshape(bs, seq_s, d)

    new_ssd_fwd = jnp.stack(new_states[0], axis=1)
    new_ssd_bwd = jnp.stack(new_states[1], axis=1)
    new_lru_fwd = jnp.concatenate(new_states[2], axis=1)
    new_lru_bwd = jnp.concatenate(new_states[3], axis=1)
    return (y_p, y_s, new_ssd_fwd, new_ssd_bwd, new_lru_fwd, new_lru_bwd)
```

```python
import functools

import jax
import jax.numpy as jnp
from jax import lax
from jax.experimental import pallas as pl
from jax.experimental.pallas import tpu as pltpu

F32 = jnp.float32
BF16 = jnp.bfloat16

LANES = 128
SUBLANES = 8
MXU_DIM = 256
VMEM_LIMIT = 56 * 1024 * 1024

GRID_W = 64
SSD_CHUNK = 128
LRU_C = 8.0
LN_EPS = 1e-5
EC_CAPACITY_FACTOR = 2
N_MOD = 6
MOD_ROWS = 8


def _pick(n, pref, align):
    best = None
    t = align
    while t <= min(n, pref):
        if n % t == 0:
            best = t
        t += align
    return best if best is not None else n


def _cparams(*sem):
    return pltpu.CompilerParams(dimension_semantics=sem, vmem_limit_bytes=VMEM_LIMIT)


def _resident(shape, index_map):
    return pl.BlockSpec(shape, index_map, pipeline_mode=pl.Buffered(1))


def _mod_row(i, tm, base, rows_per_mod):
    return base + (i * tm) // rows_per_mod


def _dot(a, b):
    return jnp.dot(a, b, preferred_element_type=F32)


def _mod_kernel(c_ref, w_ref, b_ref, o_ref):
    cs = jax.nn.silu(c_ref[...]).astype(BF16)
    o_ref[...] = _dot(cs, w_ref[...].astype(BF16)) + b_ref[...]


def _mod_call(cv, w_mod, b_mod):
    d, n = w_mod.shape
    tn = _pick(n, 512, LANES)
    return pl.pallas_call(
        _mod_kernel,
        grid=(n // tn,),
        in_specs=[pl.BlockSpec((MOD_ROWS, d), lambda j: (0, 0)),
                  pl.BlockSpec((d, tn), lambda j: (0, j)),
                  pl.BlockSpec((1, tn), lambda j: (0, j))],
        out_specs=pl.BlockSpec((MOD_ROWS, tn), lambda j: (0, j)),
        out_shape=jax.ShapeDtypeStruct((MOD_ROWS, n), F32),
        compiler_params=_cparams("parallel"),
        name="mod",
    )(cv, w_mod, b_mod)


def _modulate_kernel(x_ref, sh_ref, sc_ref, o_ref, *, tm, base, rpm):
    r = _mod_row(pl.program_id(0), tm, base, rpm)
    sc = sc_ref[pl.ds(r, 1), :]
    sh = sh_ref[pl.ds(r, 1), :]
    o_ref[...] = (x_ref[...] * (1.0 + sc) + sh).astype(o_ref.dtype)


def _modulate_call(x, mod, base, rpm, q_shift, q_scale):
    t, d = x.shape
    tm = _pick(t, 256, SUBLANES)
    assert rpm % tm == 0
    return pl.pallas_call(
        functools.partial(_modulate_kernel, tm=tm, base=base, rpm=rpm),
        grid=(t // tm,),
        in_specs=[pl.BlockSpec((tm, d), lambda i: (i, 0)),
                  pl.BlockSpec((MOD_ROWS, d), lambda i: (0, q_shift)),
                  pl.BlockSpec((MOD_ROWS, d), lambda i: (0, q_scale))],
        out_specs=pl.BlockSpec((tm, d), lambda i: (i, 0)),
        out_shape=jax.ShapeDtypeStruct((t, d), BF16),
        compiler_params=_cparams("parallel"),
        name="modulate",
    )(x, mod, mod)


def _proj_in_kernel(x_ref, w_ref, o_ref):
    o_ref[...] = _dot(x_ref[...], w_ref[...].astype(BF16))


def _proj_in_conv_kernel(x_ref, w_ref, cw_ref, cb_ref, o_ref, *, seq):
    acc = _dot(x_ref[...], w_ref[...].astype(BF16))
    o_ref[...] = jax.nn.silu(_conv4(acc, cw_ref[...], cb_ref[...], seq))


def _proj_in_call(h, w, skip_off, skip_width):
    t, k = h.shape
    n = w.shape[1] - skip_width
    tm = _pick(t, 2048, SUBLANES)
    tn = _pick(n, MXU_DIM, LANES)
    assert skip_off % tn == 0 and skip_width % tn == 0
    j_skip, n_skip = skip_off // tn, skip_width // tn
    return pl.pallas_call(
        _proj_in_kernel,
        grid=(t // tm, n // tn),
        in_specs=[_resident((tm, k), lambda i, j: (i, 0)),
                  pl.BlockSpec((k, tn), lambda i, j: (0, jnp.where(j < j_skip, j, j + n_skip)))],
        out_specs=pl.BlockSpec((tm, tn), lambda i, j: (i, j)),
        out_shape=jax.ShapeDtypeStruct((t, n), F32),
        compiler_params=_cparams("parallel", "arbitrary"),
        name="proj_in",
    )(h, w)


def _proj_in_conv_call(h, w, col_off, conv_w, conv_b, seq):
    t, k = h.shape
    width = conv_w.shape[1]
    tm = _pick(t, 2048, SUBLANES)
    tn = _pick(width, MXU_DIM, LANES)
    assert tm % seq == 0 and conv_w.shape[0] == 4 and col_off % tn == 0
    jo = col_off // tn
    return pl.pallas_call(
        functools.partial(_proj_in_conv_kernel, seq=seq),
        grid=(t // tm, width // tn),
        in_specs=[_resident((tm, k), lambda i, j: (i, 0)),
                  pl.BlockSpec((k, tn), lambda i, j: (0, jo + j)),
                  pl.BlockSpec((4, tn), lambda i, j: (0, j)), pl.BlockSpec((1, tn), lambda i, j: (0, j))],
        out_specs=pl.BlockSpec((tm, tn), lambda i, j: (i, j)),
        out_shape=jax.ShapeDtypeStruct((t, width), F32),
        compiler_params=_cparams("parallel", "arbitrary"),
        name="proj_in_conv",
    )(h, w, conv_w, conv_b)


def _proj_ssd_kernel(y_ref, w_ref, lg_ref, o_ref):
    acc = _dot(y_ref[...], w_ref[...].astype(BF16))
    o_ref[...] = jax.nn.sigmoid(lg_ref[...]) * acc


def _proj_ssd_call(y, w, proj, logit_off):
    t, k = y.shape
    n = w.shape[1]
    tm = _pick(t, 1024, SUBLANES)
    tn = _pick(n, MXU_DIM, LANES)
    assert logit_off % tn == 0
    lo = logit_off // tn
    return pl.pallas_call(
        _proj_ssd_kernel,
        grid=(t // tm, n // tn),
        in_specs=[_resident((tm, k), lambda i, j: (i, 0)),
                  pl.BlockSpec((k, tn), lambda i, j: (0, j)),
                  pl.BlockSpec((tm, tn), lambda i, j: (i, lo + j))],
        out_specs=pl.BlockSpec((tm, tn), lambda i, j: (i, j)),
        out_shape=jax.ShapeDtypeStruct((t, n), F32),
        compiler_params=_cparams("parallel", "arbitrary"),
        name="proj_ssd",
    )(y, w, proj)


def _proj_lru_kernel(h_ref, w_ref, lg_ref, s_ref, o_ref):
    acc = _dot(h_ref[...], w_ref[...].astype(BF16))
    o_ref[...] = (s_ref[...] + jax.nn.sigmoid(lg_ref[...]) * acc).astype(o_ref.dtype)


def _proj_lru_call(hl, w, proj, logit_off, branch_ssd):
    t, k = hl.shape
    n = w.shape[1]
    tm = _pick(t, 2048, SUBLANES)
    tn = _pick(n, MXU_DIM, LANES)
    assert logit_off % tn == 0
    lo = logit_off // tn
    return pl.pallas_call(
        _proj_lru_kernel,
        grid=(t // tm, n // tn),
        in_specs=[_resident((tm, k), lambda i, j: (i, 0)),
                  pl.BlockSpec((k, tn), lambda i, j: (0, j)),
                  pl.BlockSpec((tm, tn), lambda i, j: (i, lo + j)),
                  pl.BlockSpec((tm, tn), lambda i, j: (i, j))],
        out_specs=pl.BlockSpec((tm, tn), lambda i, j: (i, j)),
        out_shape=jax.ShapeDtypeStruct((t, n), BF16),
        compiler_params=_cparams("parallel", "arbitrary"),
        name="proj_lru",
    )(hl, w, proj, branch_ssd)


def _proj_out_kernel(m_ref, w_ref, x_ref, g_ref, o_ref, *, tm, base, rpm, alpha):
    r = _mod_row(pl.program_id(0), tm, base, rpm)
    acc = _dot(m_ref[...], w_ref[...].astype(BF16))
    o_ref[...] = alpha * x_ref[...] + g_ref[pl.ds(r, 1), :] * acc


def _proj_out_call(merged, w, x, mod, base, rpm, q_gate, alpha):
    t, k = merged.shape
    n = w.shape[1]
    tm = _pick(min(t, rpm), 2048, SUBLANES)
    tn = _pick(n, MXU_DIM, LANES)
    gq = q_gate * (n // tn)
    return pl.pallas_call(
        functools.partial(_proj_out_kernel, tm=tm, base=base, rpm=rpm, alpha=alpha),
        grid=(t // tm, n // tn),
        in_specs=[_resident((tm, k), lambda i, j: (i, 0)),
                  pl.BlockSpec((k, tn), lambda i, j: (0, j)),
                  pl.BlockSpec((tm, tn), lambda i, j: (i, j)),
                  pl.BlockSpec((MOD_ROWS, tn), lambda i, j: (0, gq + j))],
        out_specs=pl.BlockSpec((tm, tn), lambda i, j: (i, j)),
        out_shape=jax.ShapeDtypeStruct((t, n), F32),
        compiler_params=_cparams("parallel", "arbitrary"),
        name="proj_out",
    )(merged, w, x, mod)


def _shift_rows(x, d, reverse, fill, period=None):
    n = x.shape[0]
    if (period is None or period == n) and d % SUBLANES == 0:
        pad = jnp.full((d,) + x.shape[1:], fill, x.dtype)
        return jnp.concatenate([x[d:], pad] if reverse else [pad, x[:n - d]], axis=0)
    row = lax.broadcasted_iota(jnp.int32, x.shape, 0)
    if period is None or period == n:
        period = n
    else:
        row = row % period
    if reverse:
        return jnp.where(row < period - d, pltpu.roll(x, n - d, 0), fill)
    return jnp.where(row >= d, pltpu.roll(x, d, 0), fill)


def _cumsum_rows(a, reverse):
    d = 1
    while d < a.shape[0]:
        a = a + _shift_rows(a, d, reverse, 0.0)
        d *= 2
    return a


def _doubling_scan(a, u, reverse, period=None):
    span = a.shape[0] if period is None else period
    d = 1
    while d < span:
        u = a * _shift_rows(u, d, reverse, 0.0, period) + u
        a = a * _shift_rows(a, d, reverse, 1.0, period)
        d *= 2
    return a, u


def _scan_block(n):
    bk = SUBLANES
    while bk * bk < n:
        bk *= 2
    return bk


def _linear_scan(a, u, reverse, seq, a_scr, u_scr):
    n = a.shape[0]
    bk = _scan_block(seq)
    nb = n // bk
    if seq // bk < 2 or seq % bk:
        return _doubling_scan(a, u, reverse, seq)[1]
    a, u = _doubling_scan(a, u, reverse, bk)
    edge = 0 if reverse else bk - 1

    def block_edges(x, scr):
        for c in range(scr.shape[0]):
            scr[c] = x[:, c * LANES:(c + 1) * LANES]
        return jnp.concatenate([scr[c, pl.ds(edge, nb, stride=bk), :] for c in range(scr.shape[0])], axis=1)

    a_edge, h_edge = _doubling_scan(block_edges(a, a_scr), block_edges(u, u_scr), reverse, seq // bk)
    carry = _shift_rows(h_edge, 1, reverse, 0.0, seq // bk)
    return jnp.concatenate([u[j * bk:(j + 1) * bk] + a[j * bk:(j + 1) * bk] * carry[j:j + 1, :]
                            for j in range(nb)], axis=0)


def _conv4(x, w, b, period=None):
    return (_shift_rows(x, 1, False, 0.0, period) * w[0:1, :] + x * w[1:2, :]
            + _shift_rows(x, 1, True, 0.0, period) * w[2:3, :]
            + _shift_rows(x, 2, True, 0.0, period) * w[3:4, :] + b)


def _ssd_chunk(reverse, g, xs_ref, b_ref, c_ref, dt_ref, dtb_ref, alog_ref, h_scr, *, hpg, hd, heads):
    q = SSD_CHUNK
    dt = jax.nn.softplus(dt_ref[...] + dtb_ref[0])
    a = dt * (-jnp.exp(alog_ref[0]))
    acum = _cumsum_rows(a, reverse)
    shift = jnp.where(g == 0, 0, heads - g * hpg)
    acum = pltpu.roll(acum, shift, 1)
    dt = pltpu.roll(dt, shift, 1)
    acum_t = acum.T
    cb16 = c_ref[...].astype(BF16)
    cb = lax.dot_general(cb16, b_ref[...].astype(BF16), (((1,), (1,)), ((), ())),
                         preferred_element_type=F32)
    bt16 = b_ref[...].T.astype(BF16)
    li = lax.broadcasted_iota(jnp.int32, (q, q), 0)
    si = lax.broadcasted_iota(jnp.int32, (q, q), 1)
    cbm = jnp.where((si >= li) if reverse else (si <= li), cb, 0.0)
    last = 0 if reverse else q - 1
    h_prev = h_scr[...]
    y_off = _dot(cb16, h_prev.astype(BF16))
    pw = 2 * hd
    lo = lax.broadcasted_iota(jnp.int32, (q, pw), 1) < hd
    ys = []
    for j in range(hpg // 2):
        k0, k1 = 2 * j, 2 * j + 1
        xp = xs_ref[:, j * pw:(j + 1) * pw]
        ac0 = jnp.broadcast_to(acum[:, k0:k0 + 1], (q, q))
        ac1 = jnp.broadcast_to(acum[:, k1:k1 + 1], (q, q))

        def scores(ac, k):
            return (cbm * jnp.exp(jnp.minimum(ac - acum_t[k:k + 1, :], 0.0))).astype(BF16)

        s2 = jnp.concatenate([scores(ac0, k0), scores(ac1, k1)], axis=1)
        xdt = xp * jnp.where(lo, dt[:, k0:k0 + 1], dt[:, k1:k1 + 1])
        rhs = jnp.concatenate([jnp.where(lo, xdt, 0.0), jnp.where(lo, 0.0, xdt)], axis=0).astype(BF16)
        y_diag = _dot(s2, rhs)
        a_pair = jnp.where(lo, ac0, ac1)
        e_pair = jnp.exp(a_pair)
        ys.append(y_diag + y_off[:, j * pw:(j + 1) * pw] * e_pair)
        w_pair = jnp.exp(a_pair[last:last + 1, :] - a_pair)
        st = _dot(bt16, (xdt * w_pair).astype(BF16))
        h_scr[:, j * pw:(j + 1) * pw] = h_prev[:, j * pw:(j + 1) * pw] * e_pair[last:last + 1, :] + st
    return jnp.concatenate(ys, axis=1)


def _ssd_load_state(h0_ref, h_scr, hpg, hd):
    pw = 2 * hd
    for j in range(hpg // 2):
        pair = jnp.concatenate([h0_ref[0, 2 * j], h0_ref[0, 2 * j + 1]], axis=0)
        h_scr[:, j * pw:(j + 1) * pw] = pair.T


def _ssd_store_state(s_ref, h_scr, hpg, hd):
    pw = 2 * hd
    for j in range(hpg // 2):
        pair = h_scr[:, j * pw:(j + 1) * pw].T
        s_ref[0, 2 * j] = pair[:hd]
        s_ref[0, 2 * j + 1] = pair[hd:]


def _ssd_kernel(*refs, nc, hpg, hd, heads, has_state):
    if has_state:
        (xs_ref, b_ref, c_ref, dt_ref, z_ref, dtb_ref, alog_ref, dsk_ref, nw_ref, h0f_ref, h0b_ref,
         y_ref, sf_ref, sb_ref, yf_scr, h_scr) = refs
    else:
        (xs_ref, b_ref, c_ref, dt_ref, z_ref, dtb_ref, alog_ref, dsk_ref, nw_ref,
         y_ref, sf_ref, sb_ref, yf_scr, h_scr) = refs
        h0f_ref = h0b_ref = None
    g = pl.program_id(1)
    step = pl.program_id(2)
    q = SSD_CHUNK
    chunk = functools.partial(_ssd_chunk, g=g, xs_ref=xs_ref, b_ref=b_ref, c_ref=c_ref, dt_ref=dt_ref,
                              dtb_ref=dtb_ref, alog_ref=alog_ref, h_scr=h_scr, hpg=hpg, hd=hd, heads=heads)

    def init(h0_ref):
        if has_state:
            _ssd_load_state(h0_ref, h_scr, hpg, hd)
        else:
            h_scr[...] = jnp.zeros_like(h_scr)

    @pl.when(step == 0)
    def _():
        init(h0f_ref)

    @pl.when(step == nc)
    def _():
        init(h0b_ref)

    @pl.when(step < nc)
    def _():
        row0 = pl.multiple_of(step * q, q)
        yf_scr[pl.ds(row0, q), :] = chunk(False)

    @pl.when(step == nc - 1)
    def _():
        _ssd_store_state(sf_ref, h_scr, hpg, hd)

    @pl.when(step >= nc)
    def _():
        row0 = pl.multiple_of((2 * nc - 1 - step) * q, q)
        y = yf_scr[pl.ds(row0, q), :] + chunk(True) + dsk_ref[...] * xs_ref[...]
        y = y * jax.nn.silu(z_ref[...])
        y = y * lax.rsqrt(jnp.mean(jnp.square(y), axis=-1, keepdims=True) + LN_EPS)
        y_ref[...] = (y * nw_ref[...]).astype(y_ref.dtype)

    @pl.when(step == 2 * nc - 1)
    def _():
        _ssd_store_state(sb_ref, h_scr, hpg, hd)


def _ssd_call(z, xbc, rest, dt_off, dt_bias, a_log, d_skip_x, norm_w, nb, seq, d_ssd, groups, hd, nstate, h0):
    q = SSD_CHUNK
    nc = seq // q
    gw = d_ssd // groups
    heads = d_ssd // hd
    hpg = heads // groups
    assert seq % q == 0 and hpg % 2 == 0 and 2 * hd == LANES and nstate == LANES and heads == LANES
    assert dt_off % heads == 0
    dto = dt_off // heads
    xo = 0
    bo = d_ssd // nstate

    def chunk_of(s):
        return jnp.where(s < nc, s, 2 * nc - 1 - s)

    def out_chunk(s):
        return jnp.where(s < nc, nc - 1, 2 * nc - 1 - s)

    in_specs = [
        pl.BlockSpec((q, gw), lambda b, g, s: (b * nc + chunk_of(s), xo + g)),
        pl.BlockSpec((q, nstate), lambda b, g, s: (b * nc + chunk_of(s), bo + g)),
        pl.BlockSpec((q, nstate), lambda b, g, s: (b * nc + chunk_of(s), bo + groups + g)),
        pl.BlockSpec((q, heads), lambda b, g, s: (b * nc + chunk_of(s), dto + s // nc)),
        pl.BlockSpec((q, gw), lambda b, g, s: (b * nc + out_chunk(s), g)),
        pl.BlockSpec((1, 1, heads), lambda b, g, s: (s // nc, 0, 0)),
        pl.BlockSpec((1, 1, heads), lambda b, g, s: (s // nc, 0, 0)),
        pl.BlockSpec((1, gw), lambda b, g, s: (0, g)),
        pl.BlockSpec((1, gw), lambda b, g, s: (0, g)),
    ]
    args = [xbc, xbc, xbc, rest, z, dt_bias, a_log, d_skip_x, norm_w]
    state_spec = pl.BlockSpec((1, hpg, hd, nstate), lambda b, g, s: (b, g, 0, 0))
    if h0 is not None:
        in_specs += [state_spec, state_spec]
        args += [h0[0], h0[1]]
    state_shape = jax.ShapeDtypeStruct((nb, heads, hd, nstate), F32)
    return pl.pallas_call(
        functools.partial(_ssd_kernel, nc=nc, hpg=hpg, hd=hd, heads=heads, has_state=h0 is not None),
        grid=(nb, groups, 2 * nc),
        in_specs=in_specs,
        out_specs=[pl.BlockSpec((q, gw), lambda b, g, s: (b * nc + out_chunk(s), g)), state_spec, state_spec],
        out_shape=[jax.ShapeDtypeStruct((nb * seq, d_ssd), BF16), state_shape, state_shape],
        scratch_shapes=[pltpu.VMEM((seq, gw), F32), pltpu.VMEM((nstate, gw), F32)],
        compiler_params=_cparams("parallel", "parallel", "arbitrary"),
        name="ssd",
    )(*args)


def _lru_dir(x, xb, wa_ref, wx_ref, ba_ref, bx_ref, lam_ref, h0_ref, reverse, seq, a_scr, u_scr):
    r = jax.nn.sigmoid(_dot(xb, wa_ref[0].astype(BF16)) + ba_ref[...])
    i = jax.nn.sigmoid(_dot(xb, wx_ref[0].astype(BF16)) + bx_ref[...])
    log_a = -LRU_C * jax.nn.softplus(-lam_ref[...]) * r
    a = jnp.exp(log_a)
    u = jnp.sqrt(-jnp.tanh(log_a) * (a * a + 1.0)) * (i * x)
    n = x.shape[0]
    first = seq - 1 if reverse else 0
    row = lax.broadcasted_iota(jnp.int32, x.shape, 0) % seq
    h0_rows = jnp.concatenate([jnp.broadcast_to(h0_ref[s], (seq, x.shape[1])) for s in range(n // seq)], axis=0)
    u = u + jnp.where(row == first, a * h0_rows, 0.0)
    return _linear_scan(a, u, reverse, seq, a_scr, u_scr)


def _lru_kernel(x_ref, cw_ref, cb_ref, waf_ref, wxf_ref, baf_ref, bxf_ref, lamf_ref,
                wab_ref, wxb_ref, bab_ref, bxb_ref, lamb_ref, h0f_ref, h0b_ref,
                hr_ref, sf_ref, sb_ref, af_scr, uf_scr, ab_scr, ub_scr, *, seq):
    x = _conv4(x_ref[...], cw_ref[...], cb_ref[...], seq)
    xb = x.astype(BF16)
    h_f = _lru_dir(x, xb, waf_ref, wxf_ref, baf_ref, bxf_ref, lamf_ref, h0f_ref, False, seq, af_scr, uf_scr)
    h_b = _lru_dir(x, xb, wab_ref, wxb_ref, bab_ref, bxb_ref, lamb_ref, h0b_ref, True, seq, ab_scr, ub_scr)
    hr_ref[...] = h_f + h_b
    for s in range(sf_ref.shape[0]):
        sf_ref[s] = h_f[(s + 1) * seq - 1:(s + 1) * seq, :]
        sb_ref[s] = h_b[s * seq:s * seq + 1, :]


def _lru_call(xsrc, col_off, p, h0f, h0b, nb, seq):
    blocks, bw, _ = p['lru_wa_fwd'].shape
    d = blocks * bw
    assert col_off % bw == 0
    co = col_off // bw
    spb = _pick(nb, max(1, 2048 // seq), 1)
    rows = spb * seq
    vec = pl.BlockSpec((1, bw), lambda b, j: (0, j))
    wblk = pl.BlockSpec((1, bw, bw), lambda b, j: (j, 0, 0))
    st = pl.BlockSpec((spb, 1, bw), lambda b, j: (b, 0, j))
    st_shape = jax.ShapeDtypeStruct((nb, 1, d), F32)
    return pl.pallas_call(
        functools.partial(_lru_kernel, seq=seq),
        grid=(nb // spb, blocks),
        in_specs=[pl.BlockSpec((rows, bw), lambda b, j: (b, co + j)),
                  pl.BlockSpec((4, bw), lambda b, j: (0, j)), vec,
                  wblk, wblk, vec, vec, vec, wblk, wblk, vec, vec, vec, st, st],
        out_specs=[pl.BlockSpec((rows, bw), lambda b, j: (b, j)), st, st],
        out_shape=[jax.ShapeDtypeStruct((nb * seq, d), F32), st_shape, st_shape],
        scratch_shapes=[pltpu.VMEM((bw // LANES, rows, LANES), F32)] * 4,
        compiler_params=_cparams("parallel", "parallel"),
        name="rglru",
    )(xsrc, p['conv_lru_w'], p['conv_lru_b'],
      p['lru_wa_fwd'], p['lru_wx_fwd'], p['lru_ba_fwd'], p['lru_bx_fwd'], p['lru_lam_fwd'],
      p['lru_wa_bwd'], p['lru_wx_bwd'], p['lru_ba_bwd'], p['lru_bx_bwd'], p['lru_lam_bwd'],
      h0f, h0b)


def _gate_mul_kernel(g_ref, h_ref, o_ref):
    o_ref[...] = (jax.nn.gelu(g_ref[...]) * h_ref[...]).astype(o_ref.dtype)


def _gate_mul_call(proj, gate_off, hr):
    t, d = hr.shape
    tm = _pick(t, 1024, SUBLANES)
    tn = _pick(d, MXU_DIM, LANES)
    assert gate_off % tn == 0
    go = gate_off // tn
    return pl.pallas_call(
        _gate_mul_kernel,
        grid=(t // tm, d // tn),
        in_specs=[pl.BlockSpec((tm, tn), lambda i, j: (i, go + j)),
                  pl.BlockSpec((tm, tn), lambda i, j: (i, j))],
        out_specs=pl.BlockSpec((tm, tn), lambda i, j: (i, j)),
        out_shape=jax.ShapeDtypeStruct((t, d), BF16),
        compiler_params=_cparams("parallel", "parallel"),
        name="lru_gate",
    )(proj, hr)


def _layer_norm(v, g, b):
    mu = jnp.mean(v, axis=-1, keepdims=True)
    var = jnp.mean(jnp.square(v - mu), axis=-1, keepdims=True)
    return (v - mu) * lax.rsqrt(var + LN_EPS) * g + b


def _ln_router_kernel(v_ref, g_ref, b_ref, sh_ref, sc_ref, wr_ref, x1_ref, h2_ref, aff_ref,
                      *, tm, base, rpm, n_exp):
    r = _mod_row(pl.program_id(0), tm, base, rpm)
    x1 = _layer_norm(v_ref[...], g_ref[...], b_ref[...])
    x1_ref[...] = x1
    h2 = (x1 * (1.0 + sc_ref[pl.ds(r, 1), :]) + sh_ref[pl.ds(r, 1), :]).astype(BF16)
    h2_ref[...] = h2
    logits = _dot(h2, wr_ref[...].astype(BF16))
    lane = lax.broadcasted_iota(jnp.int32, logits.shape, 1)
    logits = jnp.where(lane < n_exp, logits, -jnp.inf)
    e = jnp.exp(logits - jnp.max(logits, axis=-1, keepdims=True))
    aff_ref[...] = e / jnp.sum(e, axis=-1, keepdims=True)


def _ln_router_call(v, ln_g, ln_b, mod, base, rpm, q_shift, q_scale, w_router_pad, n_exp):
    t, d = v.shape
    tm = _pick(t, 256, SUBLANES)
    assert rpm % tm == 0
    row = pl.BlockSpec((tm, d), lambda i: (i, 0))
    vec = pl.BlockSpec((1, d), lambda i: (0, 0))
    return pl.pallas_call(
        functools.partial(_ln_router_kernel, tm=tm, base=base, rpm=rpm, n_exp=n_exp),
        grid=(t // tm,),
        in_specs=[row, vec, vec,
                  pl.BlockSpec((MOD_ROWS, d), lambda i: (0, q_shift)),
                  pl.BlockSpec((MOD_ROWS, d), lambda i: (0, q_scale)),
                  pl.BlockSpec((d, LANES), lambda i: (0, 0))],
        out_specs=[row, row, pl.BlockSpec((tm, LANES), lambda i: (i, 0))],
        out_shape=[jax.ShapeDtypeStruct((t, d), F32), jax.ShapeDtypeStruct((t, d), BF16),
                   jax.ShapeDtypeStruct((t, LANES), F32)],
        compiler_params=_cparams("parallel"),
        name="ln1_router",
    )(v, ln_g, ln_b, mod, mod, w_router_pad)


def _experts_per_step(n, n_exp):
    return n_exp if n <= 512 else 1


def _rank_kernel(aff_ref, rank_ref, arow_ref, afft_scr, *, n, eps):
    eb = pl.program_id(1)

    @pl.when(eb == 0)
    def _():
        afft_scr[...] = aff_ref[...].T

    rb = LANES
    t_row = lax.broadcasted_iota(jnp.int32, (rb, n), 1)
    lane = lax.broadcasted_iota(jnp.int32, (rb, LANES), 1)
    for idx in range(eps):
        e = eb * eps + idx
        a_row = afft_scr[pl.ds(e, 1), :]
        arow_ref[0, idx] = a_row

        def body(c, cnt, e=e, a_row=a_row):
            r0 = pl.multiple_of(c * rb, rb)
            blk = aff_ref[pl.ds(r0, rb), :]
            a_col = jnp.sum(jnp.where(lane == e, blk, 0.0), axis=1, keepdims=True)
            t_col = r0 + lax.broadcasted_iota(jnp.int32, (rb, n), 0)
            beats = jnp.where(a_col > a_row, 1.0,
                              jnp.where(a_col == a_row, jnp.where(t_col < t_row, 1.0, 0.0), 0.0))
            return cnt + jnp.sum(beats, axis=0, keepdims=True)

        rank_ref[0, idx] = lax.fori_loop(0, n // rb, body, jnp.zeros((1, n), F32))


def _rank_call(aff, nsets, n, n_exp):
    eps = _experts_per_step(n, n_exp)
    out = jax.ShapeDtypeStruct((nsets, n_exp, 1, n), F32)
    spec = pl.BlockSpec((1, eps, 1, n), lambda s, e: (s, e, 0, 0))
    return pl.pallas_call(
        functools.partial(_rank_kernel, n=n, eps=eps),
        grid=(nsets, n_exp // eps),
        in_specs=[pl.BlockSpec((n, LANES), lambda s, e: (s, 0))],
        out_specs=[spec, spec],
        out_shape=[out, out],
        scratch_shapes=[pltpu.VMEM((LANES, n), F32)],
        compiler_params=_cparams("parallel", "arbitrary"),
        name="ec_rank",
    )(aff)


def _gather_kernel(rank_ref, arow_ref, h_ref, hs_ref, gate_ref, *, cap, eps):
    n, d = h_ref.shape
    slot = lax.broadcasted_iota(jnp.int32, (cap, n), 0).astype(F32)
    sels = [rank_ref[0, idx] == slot for idx in range(eps)]
    onehot = jnp.concatenate([jnp.where(s, 1.0, 0.0) for s in sels], axis=0).astype(BF16)
    hs_ref[...] = _dot(onehot, h_ref[...]).reshape(eps, cap, d).astype(hs_ref.dtype)
    for idx in range(eps):
        gate = jnp.sum(jnp.where(sels[idx], arow_ref[0, idx], 0.0), axis=1, keepdims=True)
        gate_ref[idx] = jnp.broadcast_to(gate, (cap, LANES))


def _gather_call(rank, arow, h2, nsets, n, cap, n_exp):
    d = h2.shape[1]
    eps = _experts_per_step(n, n_exp)
    rspec = pl.BlockSpec((1, eps, 1, n), lambda s, e: (s, e, 0, 0))
    return pl.pallas_call(
        functools.partial(_gather_kernel, cap=cap, eps=eps),
        grid=(nsets, n_exp // eps),
        in_specs=[rspec, rspec, _resident((n, d), lambda s, e: (s, 0))],
        out_specs=[pl.BlockSpec((eps, cap, d), lambda s, e: (e, s, 0)),
                   pl.BlockSpec((eps, cap, LANES), lambda s, e: (e, s, 0))],
        out_shape=[jax.ShapeDtypeStruct((n_exp, nsets * cap, d), BF16),
                   jax.ShapeDtypeStruct((n_exp, nsets * cap, LANES), F32)],
        compiler_params=_cparams("parallel", "arbitrary"),
        name="ec_gather",
    )(rank, arow, h2)


def _dot_nt(a, b):
    return lax.dot_general(a, b, (((1,), (1,)), ((), ())), preferred_element_type=F32)


def _ffn_up_kernel(xp_ref, xs_ref, w1_ref, w3_ref, o_ref, *, d_ff, tn):
    w1 = w1_ref[0].astype(BF16)
    w3 = w3_ref[0].astype(BF16)
    mp = xp_ref.shape[1]
    j = pl.program_id(1)
    for x_ref, rows in ((xp_ref, slice(0, mp)), (xs_ref, slice(mp, None))):
        x = x_ref[0]
        u = _dot_nt(x, w1)
        v = _dot_nt(x, w3)
        col = j * tn + lax.broadcasted_iota(jnp.int32, u.shape, 1)
        o_ref[0, rows, :] = jnp.where(col < d_ff, jax.nn.silu(u) * v, 0.0).astype(o_ref.dtype)


def _ffn_up_call(hsel_p, hsel_s, w1t, w3t, f_pad):
    n_exp, mp, d = hsel_p.shape
    ms = hsel_s.shape[1]
    d_ff = w1t.shape[1]
    tn = MXU_DIM
    assert f_pad % tn == 0
    wspec = pl.BlockSpec((1, tn, d), lambda e, j: (e, j, 0))
    return pl.pallas_call(
        functools.partial(_ffn_up_kernel, d_ff=d_ff, tn=tn),
        grid=(n_exp, f_pad // tn),
        in_specs=[_resident((1, mp, d), lambda e, j: (e, 0, 0)),
                  _resident((1, ms, d), lambda e, j: (e, 0, 0)), wspec, wspec],
        out_specs=pl.BlockSpec((1, mp + ms, tn), lambda e, j: (e, 0, j)),
        out_shape=jax.ShapeDtypeStruct((n_exp, mp + ms, f_pad), BF16),
        compiler_params=_cparams("parallel", "arbitrary"),
        name="ffn_up",
    )(hsel_p, hsel_s, w1t, w3t)


def _ffn_down_kernel(a_ref, w_ref, gate_ref, hi_ref, lo_ref, acc_scr, *, kh, d_ff):
    k = pl.program_id(3)

    @pl.when(k == 0)
    def _():
        acc_scr[...] = _dot(a_ref[0, :, :kh], w_ref[0].astype(BF16))

    @pl.when(k == 1)
    def _():
        row = lax.broadcasted_iota(jnp.int32, w_ref.shape[1:], 0)
        w = jnp.where(row < d_ff - kh, w_ref[0], 0.0).astype(BF16)
        og = (acc_scr[...] + _dot(a_ref[0, :, kh:], w)) * gate_ref[0][:, :1]
        hi = og.astype(BF16)
        hi_ref[0] = hi
        lo_ref[0] = (og - hi.astype(F32)).astype(BF16)


def _ffn_down_call(a, w2, gate):
    n_exp, m, f_pad = a.shape
    d_ff, d = w2.shape[1:]
    kh = f_pad // 2
    assert kh % LANES == 0 and kh < d_ff <= f_pad
    tm = _pick(m, 1024, SUBLANES)
    tn = _pick(d, MXU_DIM, LANES)
    ospec = pl.BlockSpec((1, tm, tn), lambda e, i, j, k: (e, i, j))
    oshape = jax.ShapeDtypeStruct((n_exp, m, d), BF16)
    return pl.pallas_call(
        functools.partial(_ffn_down_kernel, kh=kh, d_ff=d_ff),
        grid=(n_exp, m // tm, d // tn, 2),
        in_specs=[_resident((1, tm, f_pad), lambda e, i, j, k: (e, i, 0)),
                  pl.BlockSpec((1, kh, tn), lambda e, i, j, k: (e, k, j)),
                  pl.BlockSpec((1, tm, LANES), lambda e, i, j, k: (e, i, 0))],
        out_specs=[ospec, ospec],
        out_shape=[oshape, oshape],
        scratch_shapes=[pltpu.VMEM((tm, tn), F32)],
        compiler_params=_cparams("parallel", "parallel", "arbitrary", "arbitrary"),
        name="ffn_down",
    )(a, w2, gate)


def _combine_kernel(rk_ref, hi_ref, lo_ref, x_ref, g2_ref, lg_ref, lb_ref, o_ref, acc_scr,
                    *, cap, ec, nk, base, rpm, tr, tiles_per_set, alpha):
    s = pl.program_id(0)
    i = pl.program_id(1)
    k = pl.program_id(2)
    kc = ec * cap
    d = x_ref.shape[1]
    rc = jnp.minimum(rk_ref[...], float(cap)).astype(BF16)
    src = lax.broadcasted_iota(jnp.int32, (LANES, kc), 0)
    col = lax.broadcasted_iota(jnp.int32, (LANES, kc), 1)
    expand = jnp.where(src == k * ec + col // cap, 1.0, 0.0).astype(BF16)
    r_exp = _dot(rc, expand)
    slot = (lax.broadcasted_iota(jnp.int32, (1, kc), 1) % cap).astype(F32)
    sel = jnp.where(r_exp == slot, 1.0, 0.0).astype(BF16)
    part = _dot(sel, hi_ref[...].reshape(kc, d)) + _dot(sel, lo_ref[...].reshape(kc, d))

    @pl.when(k == 0)
    def _():
        acc_scr[...] = part

    @pl.when(k > 0)
    def _():
        acc_scr[...] += part

    @pl.when(k == nk - 1)
    def _():
        r = _mod_row(s * tiles_per_set + i, tr, base, rpm)
        v = alpha * x_ref[...] + g2_ref[pl.ds(r, 1), :] * acc_scr[...]
        o_ref[...] = _layer_norm(v, lg_ref[...], lb_ref[...])


def _combine_call(rank_col, og_hi, og_lo, x1, mod, base, rpm, q_gate, ln_g, ln_b,
                  nsets, n, cap, slot_off, alpha):
    n_exp, _, d = og_hi.shape
    t = x1.shape[0]
    tr = _pick(n, 256, SUBLANES)
    tps = n // tr
    ec = max(1, min(n_exp, 512 // cap))
    nk = n_exp // ec
    assert slot_off % cap == 0 and n_exp % ec == 0 and rpm % tr == 0
    so = slot_off // cap
    ospec = pl.BlockSpec((ec, cap, d), lambda s, i, k: (k, so + s, 0))
    row = pl.BlockSpec((tr, d), lambda s, i, k: (s * tps + i, 0))
    vec = pl.BlockSpec((1, d), lambda s, i, k: (0, 0))
    return pl.pallas_call(
        functools.partial(_combine_kernel, cap=cap, ec=ec, nk=nk, base=base, rpm=rpm, tr=tr,
                          tiles_per_set=tps, alpha=alpha),
        grid=(nsets, tps, nk),
        in_specs=[pl.BlockSpec((tr, LANES), lambda s, i, k: (s * tps + i, 0)),
                  ospec, ospec, row,
                  pl.BlockSpec((MOD_ROWS, d), lambda s, i, k: (0, q_gate)), vec, vec],
        out_specs=row,
        out_shape=jax.ShapeDtypeStruct((t, d), F32),
        scratch_shapes=[pltpu.VMEM((tr, d), F32)],
        compiler_params=_cparams("parallel", "parallel", "arbitrary"),
        name="ec_combine",
    )(rank_col, og_hi, og_lo, x1, mod, ln_g, ln_b)


def _colmajor(x, nb, seq):
    d = x.shape[-1]
    return x.reshape(nb, seq // GRID_W, GRID_W, d).transpose(0, 2, 1, 3).reshape(nb * seq, d)


def _raster(x, nb, seq):
    d = x.shape[-1]
    return x.reshape(nb, GRID_W, seq // GRID_W, d).transpose(0, 2, 1, 3).reshape(nb * seq, d)


def _mixer_and_route(x, mod, base, rpm, p, states, is_latent, dims):
    nb, seq, d = x.shape
    t = nb * seq
    x2 = x.reshape(t, d)
    d_ssd, groups, hd, nstate, d_lru, n_exp, alpha = (dims[k] for k in
                                                      ('d_ssd', 'groups', 'hd', 'nstate', 'd_lru', 'n_exp', 'alpha'))
    d_xbc = d_ssd + 2 * groups * nstate
    heads = d_ssd // hd
    off_dt = d_ssd
    off_gate = off_dt + 2 * heads
    off_lx = off_gate + d_lru
    off_merge = off_lx + d_lru

    h = _modulate_call(x2, mod, base, rpm, 0, 1)
    proj = _proj_in_call(h, p['w_in'], d_ssd, d_xbc)
    xbc = _proj_in_conv_call(h, p['w_in'], d_ssd, p['conv_xbc_w'], p['conv_xbc_b'], seq)

    h0 = None if states is None else (states[0], states[1])
    y, s_ssd_f, s_ssd_b = _ssd_call(proj, xbc, proj, off_dt, p['dt_bias'], p['a_log'], p['d_skip_x'],
                                    p['ssd_norm_w'], nb, seq, d_ssd, groups, hd, nstate, h0)

    if states is None:
        h0f = h0b = jnp.zeros((nb, 1, d_lru), F32)
    else:
        h0f, h0b = states[2].reshape(nb, 1, d_lru), states[3].reshape(nb, 1, d_lru)
    if is_latent:
        lx = _colmajor(lax.slice_in_dim(proj, off_lx, off_lx + d_lru, axis=1), nb, seq)
        hr, s_lru_f, s_lru_b = _lru_call(lx, 0, p, h0f, h0b, nb, seq)
        hr = _raster(hr, nb, seq)
    else:
        hr, s_lru_f, s_lru_b = _lru_call(proj, off_lx, p, h0f, h0b, nb, seq)
    hl = _gate_mul_call(proj, off_gate, hr)

    branch_ssd = _proj_ssd_call(y, p['w_proj_ssd'], proj, off_merge)
    merged = _proj_lru_call(hl, p['w_proj_lru'], proj, off_merge + d, branch_ssd)
    v1 = _proj_out_call(merged, p['w_out'], x2, mod, base, rpm, 2, alpha)
    x1, h2, aff = _ln_router_call(v1, p['ln1_g'], p['ln1_b'], mod, base, rpm, 3, 4, p['w_router_pad'], n_exp)
    rank, arow = _rank_call(aff, nb, seq, n_exp)
    return x1, h2, rank, arow, (s_ssd_f, s_ssd_b, s_lru_f, s_lru_b)


def _rank_columns(rank, nsets, n, n_exp):
    rc = rank.reshape(nsets, n_exp, n).transpose(0, 2, 1).reshape(nsets * n, n_exp)
    return jnp.pad(rc, ((0, 0), (0, LANES - n_exp)), constant_values=float(n))


def kernel(x_prompt, x_sample, state_ssd_fwd, state_ssd_bwd, state_lru_fwd, state_lru_bwd, c, c_ctx,
           w_mod, b_mod, w_in, conv_xbc_w, conv_xbc_b, dt_bias_fwd, dt_bias_bwd, a_log_fwd, a_log_bwd,
           d_skip, ssd_norm_w, w_proj_ssd, conv_lru_w, conv_lru_b,
           lru_wa_fwd, lru_ba_fwd, lru_wx_fwd, lru_bx_fwd, lru_lam_fwd,
           lru_wa_bwd, lru_ba_bwd, lru_wx_bwd, lru_bx_bwd, lru_lam_bwd,
           w_proj_lru, w_out, ln1_g, ln1_b, w_router, w1, w3, w2, ln2_g, ln2_b):
    depth, d, _ = w_mod.shape
    bp, seq_p, _ = x_prompt.shape
    bs, seq_s, _ = x_sample.shape
    heads, hd, nstate = state_ssd_fwd.shape[2:]
    d_ssd = w_proj_ssd.shape[1]
    d_xbc = conv_xbc_w.shape[2]
    groups = (d_xbc - d_ssd) // (2 * nstate)
    d_lru = w_proj_lru.shape[1]
    n_exp, _, d_ff = w1.shape[1:]
    alpha = (2.0 * depth) ** 0.25
    dims = dict(d_ssd=d_ssd, groups=groups, hd=hd, nstate=nstate, d_lru=d_lru, n_exp=n_exp, alpha=alpha)
    assert heads * hd == d_ssd and n_exp <= LANES and bs + 1 <= MOD_ROWS and seq_s % GRID_W == 0
    cap_p = EC_CAPACITY_FACTOR * seq_p // n_exp
    cap_s = EC_CAPACITY_FACTOR * seq_s // n_exp
    rows_p, rows_s = bp * cap_p, bs * cap_s
    f_pad = -(-d_ff // (2 * LANES)) * (2 * LANES)

    cv = jnp.zeros((MOD_ROWS, d), F32).at[0].set(c_ctx).at[1:1 + bs].set(c)

    y_p, y_s = x_prompt, x_sample
    new_states = ([], [], [], [])
    for layer in range(depth):
        p = dict(
            w_in=w_in[layer], conv_xbc_w=conv_xbc_w[layer], conv_xbc_b=conv_xbc_b[layer][None],
            dt_bias=jnp.stack([dt_bias_fwd[layer], dt_bias_bwd[layer]])[:, None, :],
            a_log=jnp.stack([a_log_fwd[layer], a_log_bwd[layer]])[:, None, :],
            d_skip_x=jnp.repeat(d_skip[layer], hd)[None], ssd_norm_w=ssd_norm_w[layer][None],
            w_proj_ssd=w_proj_ssd[layer], conv_lru_w=conv_lru_w[layer], conv_lru_b=conv_lru_b[layer][None],
            lru_wa_fwd=lru_wa_fwd[layer], lru_wx_fwd=lru_wx_fwd[layer],
            lru_ba_fwd=lru_ba_fwd[layer][None], lru_bx_fwd=lru_bx_fwd[layer][None],
            lru_lam_fwd=lru_lam_fwd[layer][None],
            lru_wa_bwd=lru_wa_bwd[layer], lru_wx_bwd=lru_wx_bwd[layer],
            lru_ba_bwd=lru_ba_bwd[layer][None], lru_bx_bwd=lru_bx_bwd[layer][None],
            lru_lam_bwd=lru_lam_bwd[layer][None],
            w_proj_lru=w_proj_lru[layer], w_out=w_out[layer],
            ln1_g=ln1_g[layer][None], ln1_b=ln1_b[layer][None],
            w_router_pad=jnp.pad(w_router[layer], ((0, 0), (0, LANES - n_exp))),
        )
        mod = _mod_call(cv, w_mod[layer], b_mod[layer][None])

        x1_p, h2_p, rank_p, arow_p, st = _mixer_and_route(
            y_p, mod, 0, bp * seq_p, p, None, False, dims)
        cached = (state_ssd_fwd[:, layer], state_ssd_bwd[:, layer], state_lru_fwd[:, layer], state_lru_bwd[:, layer])
        x1_s, h2_s, rank_s, arow_s, _ = _mixer_and_route(
            y_s, mod, 1, seq_s, p, cached, True, dims)
        for acc, s in zip(new_states, st):
            acc.append(s)

        hsel_p, gate_p = _gather_call(rank_p, arow_p, h2_p, bp, seq_p, cap_p, n_exp)
        hsel_s, gate_s = _gather_call(rank_s, arow_s, h2_s, bs, seq_s, cap_s, n_exp)
        act = _ffn_up_call(hsel_p, hsel_s, jnp.swapaxes(w1[layer], 1, 2), jnp.swapaxes(w3[layer], 1, 2), f_pad)
        og_hi, og_lo = _ffn_down_call(act, w2[layer], jnp.concatenate([gate_p, gate_s], axis=1))

        g2, b2 = ln2_g[layer][None], ln2_b[layer][None]
        y_p = _combine_call(_rank_columns(rank_p, bp, seq_p, n_exp), og_hi, og_lo, x1_p, mod, 0, bp * seq_p, 5,
                            g2, b2, bp, seq_p, cap_p, 0, alpha).reshape(bp, seq_p, d)
        y_s = _combine_call(_rank_columns(rank_s, bs, seq_s, n_exp), og_hi, og_lo, x1_s, mod, 1, seq_s, 5,
                            g2, b2, bs, seq_s, cap_s, rows_p, alpha).reshape(bs, seq_s, d)

    new_ssd_fwd = jnp.stack(new_states[0], axis=1)
    new_ssd_bwd = jnp.stack(new_states[1], axis=1)
    new_lru_fwd = jnp.concatenate(new_states[2], axis=1)
    new_lru_bwd = jnp.concatenate(new_states[3], axis=1)
    return (y_p, y_s, new_ssd_fwd, new_ssd_bwd, new_lru_fwd, new_lru_bwd)
```

```python
import functools

import jax
import jax.numpy as jnp
from jax import lax
from jax.experimental import pallas as pl
from jax.experimental.pallas import tpu as pltpu

F32 = jnp.float32
BF16 = jnp.bfloat16

LANES = 128
SUBLANES = 8
MXU_DIM = 256
VMEM_LIMIT = 56 * 1024 * 1024

GRID_W = 64
SSD_CHUNK = 128
LRU_C = 8.0
LN_EPS = 1e-5
EC_CAPACITY_FACTOR = 2
N_MOD = 6
MOD_ROWS = 8


def _pick(n, pref, align):
    best = None
    t = align
    while t <= min(n, pref):
        if n % t == 0:
            best = t
        t += align
    return best if best is not None else n


def _cparams(*sem):
    return pltpu.CompilerParams(dimension_semantics=sem, vmem_limit_bytes=VMEM_LIMIT)


def _resident(shape, index_map):
    return pl.BlockSpec(shape, index_map, pipeline_mode=pl.Buffered(1))


def _mod_row(i, tm, base, rows_per_mod):
    return base + (i * tm) // rows_per_mod


def _dot(a, b):
    return jnp.dot(a, b, preferred_element_type=F32)


def _mod_kernel(c_ref, w_ref, b_ref, o_ref):
    cs = jax.nn.silu(c_ref[...]).astype(BF16)
    o_ref[...] = _dot(cs, w_ref[...].astype(BF16)) + b_ref[...]


def _mod_call(cv, w_mod, b_mod):
    d, n = w_mod.shape
    tn = _pick(n, 512, LANES)
    return pl.pallas_call(
        _mod_kernel,
        grid=(n // tn,),
        in_specs=[pl.BlockSpec((MOD_ROWS, d), lambda j: (0, 0)),
                  pl.BlockSpec((d, tn), lambda j: (0, j)),
                  pl.BlockSpec((1, tn), lambda j: (0, j))],
        out_specs=pl.BlockSpec((MOD_ROWS, tn), lambda j: (0, j)),
        out_shape=jax.ShapeDtypeStruct((MOD_ROWS, n), F32),
        compiler_params=_cparams("parallel"),
        name="mod",
    )(cv, w_mod, b_mod)


def _modulate_kernel(x_ref, sh_ref, sc_ref, o_ref, *, tm, base, rpm):
    r = _mod_row(pl.program_id(0), tm, base, rpm)
    sc = sc_ref[pl.ds(r, 1), :]
    sh = sh_ref[pl.ds(r, 1), :]
    o_ref[...] = (x_ref[...] * (1.0 + sc) + sh).astype(o_ref.dtype)


def _modulate_call(x, mod, base, rpm, q_shift, q_scale):
    t, d = x.shape
    tm = _pick(t, 256, SUBLANES)
    assert rpm % tm == 0
    return pl.pallas_call(
        functools.partial(_modulate_kernel, tm=tm, base=base, rpm=rpm),
        grid=(t // tm,),
        in_specs=[pl.BlockSpec((tm, d), lambda i: (i, 0)),
                  pl.BlockSpec((MOD_ROWS, d), lambda i: (0, q_shift)),
                  pl.BlockSpec((MOD_ROWS, d), lambda i: (0, q_scale))],
        out_specs=pl.BlockSpec((tm, d), lambda i: (i, 0)),
        out_shape=jax.ShapeDtypeStruct((t, d), BF16),
        compiler_params=_cparams("parallel"),
        name="modulate",
    )(x, mod, mod)


def _proj_in_kernel(x_ref, w_ref, o_ref):
    o_ref[...] = _dot(x_ref[...], w_ref[...].astype(BF16))


def _proj_in_conv_kernel(x_ref, w_ref, cw_ref, cb_ref, o_ref, *, seq):
    acc = _dot(x_ref[...], w_ref[...].astype(BF16))
    o_ref[...] = jax.nn.silu(_conv4(acc, cw_ref[...], cb_ref[...], seq))


def _proj_in_call(h, w, skip_off, skip_width):
    t, k = h.shape
    n = w.shape[1] - skip_width
    tm = _pick(t, 2048, SUBLANES)
    tn = _pick(n, MXU_DIM, LANES)
    assert skip_off % tn == 0 and skip_width % tn == 0
    j_skip, n_skip = skip_off // tn, skip_width // tn
    return pl.pallas_call(
        _proj_in_kernel,
        grid=(t // tm, n // tn),
        in_specs=[_resident((tm, k), lambda i, j: (i, 0)),
                  pl.BlockSpec((k, tn), lambda i, j: (0, jnp.where(j < j_skip, j, j + n_skip)))],
        out_specs=pl.BlockSpec((tm, tn), lambda i, j: (i, j)),
        out_shape=jax.ShapeDtypeStruct((t, n), F32),
        compiler_params=_cparams("parallel", "arbitrary"),
        name="proj_in",
    )(h, w)


def _proj_in_conv_call(h, w, col_off, conv_w, conv_b, seq):
    t, k = h.shape
    width = conv_w.shape[1]
    tm = _pick(t, 2048, SUBLANES)
    tn = _pick(width, MXU_DIM, LANES)
    assert tm % seq == 0 and conv_w.shape[0] == 4 and col_off % tn == 0
    jo = col_off // tn
    return pl.pallas_call(
        functools.partial(_proj_in_conv_kernel, seq=seq),
        grid=(t // tm, width // tn),
        in_specs=[_resident((tm, k), lambda i, j: (i, 0)),
                  pl.BlockSpec((k, tn), lambda i, j: (0, jo + j)),
                  pl.BlockSpec((4, tn), lambda i, j: (0, j)), pl.BlockSpec((1, tn), lambda i, j: (0, j))],
        out_specs=pl.BlockSpec((tm, tn), lambda i, j: (i, j)),
        out_shape=jax.ShapeDtypeStruct((t, width), F32),
        compiler_params=_cparams("parallel", "arbitrary"),
        name="proj_in_conv",
    )(h, w, conv_w, conv_b)


def _proj_ssd_kernel(y_ref, w_ref, lg_ref, o_ref):
    acc = _dot(y_ref[...], w_ref[...].astype(BF16))
    o_ref[...] = jax.nn.sigmoid(lg_ref[...]) * acc


def _proj_ssd_call(y, w, proj, logit_off):
    t, k = y.shape
    n = w.shape[1]
    tm = _pick(t, 1024, SUBLANES)
    tn = _pick(n, MXU_DIM, LANES)
    assert logit_off % tn == 0
    lo = logit_off // tn
    return pl.pallas_call(
        _proj_ssd_kernel,
        grid=(t // tm, n // tn),
        in_specs=[_resident((tm, k), lambda i, j: (i, 0)),
                  pl.BlockSpec((k, tn), lambda i, j: (0, j)),
                  pl.BlockSpec((tm, tn), lambda i, j: (i, lo + j))],
        out_specs=pl.BlockSpec((tm, tn), lambda i, j: (i, j)),
        out_shape=jax.ShapeDtypeStruct((t, n), F32),
        compiler_params=_cparams("parallel", "arbitrary"),
        name="proj_ssd",
    )(y, w, proj)


def _proj_lru_kernel(h_ref, w_ref, lg_ref, s_ref, o_ref):
    acc = _dot(h_ref[...], w_ref[...].astype(BF16))
    o_ref[...] = (s_ref[...] + jax.nn.sigmoid(lg_ref[...]) * acc).astype(o_ref.dtype)


def _proj_lru_call(hl, w, proj, logit_off, branch_ssd):
    t, k = hl.shape
    n = w.shape[1]
    tm = _pick(t, 2048, SUBLANES)
    tn = _pick(n, MXU_DIM, LANES)
    assert logit_off % tn == 0
    lo = logit_off // tn
    return pl.pallas_call(
        _proj_lru_kernel,
        grid=(t // tm, n // tn),
        in_specs=[_resident((tm, k), lambda i, j: (i, 0)),
                  pl.BlockSpec((k, tn), lambda i, j: (0, j)),
                  pl.BlockSpec((tm, tn), lambda i, j: (i, lo + j)),
                  pl.BlockSpec((tm, tn), lambda i, j: (i, j))],
        out_specs=pl.BlockSpec((tm, tn), lambda i, j: (i, j)),
        out_shape=jax.ShapeDtypeStruct((t, n), BF16),
        compiler_params=_cparams("parallel", "arbitrary"),
        name="proj_lru",
    )(hl, w, proj, branch_ssd)


def _proj_out_kernel(m_ref, w_ref, x_ref, g_ref, o_ref, *, tm, base, rpm, alpha):
    r = _mod_row(pl.program_id(0), tm, base, rpm)
    acc = _dot(m_ref[...], w_ref[...].astype(BF16))
    o_ref[...] = alpha * x_ref[...] + g_ref[pl.ds(r, 1), :] * acc


def _proj_out_call(merged, w, x, mod, base, rpm, q_gate, alpha):
    t, k = merged.shape
    n = w.shape[1]
    tm = _pick(min(t, rpm), 2048, SUBLANES)
    tn = _pick(n, MXU_DIM, LANES)
    gq = q_gate * (n // tn)
    return pl.pallas_call(
        functools.partial(_proj_out_kernel, tm=tm, base=base, rpm=rpm, alpha=alpha),
        grid=(t // tm, n // tn),
        in_specs=[_resident((tm, k), lambda i, j: (i, 0)),
                  pl.BlockSpec((k, tn), lambda i, j: (0, j)),
                  pl.BlockSpec((tm, tn), lambda i, j: (i, j)),
                  pl.BlockSpec((MOD_ROWS, tn), lambda i, j: (0, gq + j))],
        out_specs=pl.BlockSpec((tm, tn), lambda i, j: (i, j)),
        out_shape=jax.ShapeDtypeStruct((t, n), F32),
        compiler_params=_cparams("parallel", "arbitrary"),
        name="proj_out",
    )(merged, w, x, mod)


def _shift_rows(x, d, reverse, fill, period=None):
    n = x.shape[0]
    if (period is None or period == n) and d % SUBLANES == 0:
        pad = jnp.full((d,) + x.shape[1:], fill, x.dtype)
        return jnp.concatenate([x[d:], pad] if reverse else [pad, x[:n - d]], axis=0)
    row = lax.broadcasted_iota(jnp.int32, x.shape, 0)
    if period is None or period == n:
        period = n
    else:
        row = row % period
    if reverse:
        return jnp.where(row < period - d, pltpu.roll(x, n - d, 0), fill)
    return jnp.where(row >= d, pltpu.roll(x, d, 0), fill)


def _cumsum_rows(a, reverse):
    d = 1
    while d < a.shape[0]:
        a = a + _shift_rows(a, d, reverse, 0.0)
        d *= 2
    return a


def _doubling_scan(a, u, reverse, period=None):
    span = a.shape[0] if period is None else period
    d = 1
    while d < span:
        u = a * _shift_rows(u, d, reverse, 0.0, period) + u
        a = a * _shift_rows(a, d, reverse, 1.0, period)
        d *= 2
    return a, u


def _scan_block(n):
    bk = SUBLANES
    while bk * bk < n:
        bk *= 2
    return bk


def _linear_scan(a, u, reverse, seq, a_scr, u_scr):
    n = a.shape[0]
    bk = _scan_block(seq)
    nb = n // bk
    if seq // bk < 2 or seq % bk:
        return _doubling_scan(a, u, reverse, seq)[1]
    a, u = _doubling_scan(a, u, reverse, bk)
    edge = 0 if reverse else bk - 1

    def block_edges(x, scr):
        for c in range(scr.shape[0]):
            scr[c] = x[:, c * LANES:(c + 1) * LANES]
        return jnp.concatenate([scr[c, pl.ds(edge, nb, stride=bk), :] for c in range(scr.shape[0])], axis=1)

    a_edge, h_edge = _doubling_scan(block_edges(a, a_scr), block_edges(u, u_scr), reverse, seq // bk)
    carry = _shift_rows(h_edge, 1, reverse, 0.0, seq // bk)
    return jnp.concatenate([u[j * bk:(j + 1) * bk] + a[j * bk:(j + 1) * bk] * carry[j:j + 1, :]
                            for j in range(nb)], axis=0)


def _conv4(x, w, b, period=None):
    return (_shift_rows(x, 1, False, 0.0, period) * w[0:1, :] + x * w[1:2, :]
            + _shift_rows(x, 1, True, 0.0, period) * w[2:3, :]
            + _shift_rows(x, 2, True, 0.0, period) * w[3:4, :] + b)


def _split3_packed(x, w):
    hi = x.astype(BF16).astype(F32)
    rest = x - hi
    mid = rest.astype(BF16).astype(F32)
    low = rest - mid
    lane = lax.broadcasted_iota(jnp.int32, x.shape, 1)
    packed = jnp.where(lane < w, hi, jnp.where(lane < 2 * w, pltpu.roll(mid, w, 1), pltpu.roll(low, 2 * w, 1)))
    return packed.astype(BF16)


def _head_spread_matrix(hpg, q):
    r = jnp.arange(LANES)[:, None]
    c = jnp.arange(hpg * q)[None, :]
    return jnp.where((r < 3 * hpg) & (r % hpg == c // q), 1.0, 0.0).astype(BF16)


def _ssd_chunk(reverse, r0, g, xs_ref, b_ref, c_ref, dt_ref, dtb_ref, alog_ref, spread_ref, h_scr,
               *, hpg, hd, heads):
    q = SSD_CHUNK
    rows = slice(r0, r0 + q)
    dt = jax.nn.softplus(dt_ref[rows, :] + dtb_ref[0])
    a = dt * (-jnp.exp(alog_ref[0]))
    acum = _cumsum_rows(a, reverse)
    shift = jnp.where(g == 0, 0, heads - g * hpg)
    acum = pltpu.roll(acum, shift, 1)
    dt = pltpu.roll(dt, shift, 1)
    acum_t = acum.T
    ac_all = _dot(_split3_packed(acum, hpg), spread_ref[...])
    cb16 = c_ref[rows, :].astype(BF16)
    cb = lax.dot_general(cb16, b_ref[rows, :].astype(BF16), (((1,), (1,)), ((), ())),
                         preferred_element_type=F32)
    bt16 = b_ref[rows, :].T.astype(BF16)
    li = lax.broadcasted_iota(jnp.int32, (q, q), 0)
    si = lax.broadcasted_iota(jnp.int32, (q, q), 1)
    cbm = jnp.where((si >= li) if reverse else (si <= li), cb, 0.0)
    last = 0 if reverse else q - 1
    h_prev = h_scr[...]
    y_off = _dot(cb16, h_prev.astype(BF16))
    pw = 2 * hd
    lo = lax.broadcasted_iota(jnp.int32, (q, pw), 1) < hd
    ys = []
    for j in range(hpg // 2):
        k0, k1 = 2 * j, 2 * j + 1
        xp = xs_ref[rows, j * pw:(j + 1) * pw]
        ac0 = ac_all[:, k0 * q:(k0 + 1) * q]
        ac1 = ac_all[:, k1 * q:(k1 + 1) * q]

        def scores(ac, k):
            return (cbm * jnp.exp(jnp.minimum(ac - acum_t[k:k + 1, :], 0.0))).astype(BF16)

        s2 = jnp.concatenate([scores(ac0, k0), scores(ac1, k1)], axis=1)
        xdt = xp * jnp.where(lo, dt[:, k0:k0 + 1], dt[:, k1:k1 + 1])
        rhs = jnp.concatenate([jnp.where(lo, xdt, 0.0), jnp.where(lo, 0.0, xdt)], axis=0).astype(BF16)
        y_diag = _dot(s2, rhs)
        a_pair = jnp.where(lo, ac0, ac1)
        e_pair = jnp.exp(a_pair)
        ys.append(y_diag + y_off[:, j * pw:(j + 1) * pw] * e_pair)
        w_pair = jnp.exp(a_pair[last:last + 1, :] - a_pair)
        st = _dot(bt16, (xdt * w_pair).astype(BF16))
        h_scr[:, j * pw:(j + 1) * pw] = h_prev[:, j * pw:(j + 1) * pw] * e_pair[last:last + 1, :] + st
    return jnp.concatenate(ys, axis=1)


def _ssd_load_state(h0_ref, h_scr, hpg, hd):
    pw = 2 * hd
    for j in range(hpg // 2):
        pair = jnp.concatenate([h0_ref[0, 2 * j], h0_ref[0, 2 * j + 1]], axis=0)
        h_scr[:, j * pw:(j + 1) * pw] = pair.T


def _ssd_store_state(s_ref, h_scr, hpg, hd):
    pw = 2 * hd
    for j in range(hpg // 2):
        pair = h_scr[:, j * pw:(j + 1) * pw].T
        s_ref[0, 2 * j] = pair[:hd]
        s_ref[0, 2 * j + 1] = pair[hd:]


def _ssd_kernel(*refs, ns, cps, hpg, hd, heads, has_state):
    if has_state:
        (xs_ref, b_ref, c_ref, dt_ref, z_ref, dtb_ref, alog_ref, dsk_ref, nw_ref, spread_ref, h0f_ref, h0b_ref,
         y_ref, sf_ref, sb_ref, yf_scr, h_scr) = refs
    else:
        (xs_ref, b_ref, c_ref, dt_ref, z_ref, dtb_ref, alog_ref, dsk_ref, nw_ref, spread_ref,
         y_ref, sf_ref, sb_ref, yf_scr, h_scr) = refs
        h0f_ref = h0b_ref = None
    g = pl.program_id(1)
    step = pl.program_id(2)
    q = SSD_CHUNK
    chunk = functools.partial(_ssd_chunk, g=g, xs_ref=xs_ref, b_ref=b_ref, c_ref=c_ref, dt_ref=dt_ref,
                              dtb_ref=dtb_ref, alog_ref=alog_ref, spread_ref=spread_ref, h_scr=h_scr,
                              hpg=hpg, hd=hd, heads=heads)

    def init(h0_ref):
        if has_state:
            _ssd_load_state(h0_ref, h_scr, hpg, hd)
        else:
            h_scr[...] = jnp.zeros_like(h_scr)

    @pl.when(step == 0)
    def _():
        init(h0f_ref)

    @pl.when(step == ns)
    def _():
        init(h0b_ref)

    @pl.when(step < ns)
    def _():
        for sub in range(cps):
            row0 = pl.multiple_of((step * cps + sub) * q, q)
            yf_scr[pl.ds(row0, q), :] = chunk(False, sub * q)

    @pl.when(step == ns - 1)
    def _():
        _ssd_store_state(sf_ref, h_scr, hpg, hd)

    @pl.when(step >= ns)
    def _():
        for sub in reversed(range(cps)):
            rows = slice(sub * q, (sub + 1) * q)
            row0 = pl.multiple_of(((2 * ns - 1 - step) * cps + sub) * q, q)
            y = yf_scr[pl.ds(row0, q), :] + chunk(True, sub * q) + dsk_ref[...] * xs_ref[rows, :]
            y = y * jax.nn.silu(z_ref[rows, :])
            y = y * lax.rsqrt(jnp.mean(jnp.square(y), axis=-1, keepdims=True) + LN_EPS)
            y_ref[rows, :] = (y * nw_ref[...]).astype(y_ref.dtype)

    @pl.when(step == 2 * ns - 1)
    def _():
        _ssd_store_state(sb_ref, h_scr, hpg, hd)


def _ssd_call(z, xbc, rest, dt_off, dt_bias, a_log, d_skip_x, norm_w, nb, seq, d_ssd, groups, hd, nstate, h0):
    q = SSD_CHUNK
    nc = seq // q
    gw = d_ssd // groups
    heads = d_ssd // hd
    hpg = heads // groups
    assert seq % q == 0 and hpg % 2 == 0 and 2 * hd == LANES and nstate == LANES and heads == LANES
    assert dt_off % heads == 0
    dto = dt_off // heads
    bo = d_ssd // nstate
    cps = 2 if nc % 2 == 0 else 1
    ns = nc // cps
    rq = cps * q

    def chunk_of(s):
        return jnp.where(s < ns, s, 2 * ns - 1 - s)

    def out_chunk(s):
        return jnp.where(s < ns, ns - 1, 2 * ns - 1 - s)

    in_specs = [
        pl.BlockSpec((rq, gw), lambda b, g, s: (b * ns + chunk_of(s), g)),
        pl.BlockSpec((rq, nstate), lambda b, g, s: (b * ns + chunk_of(s), bo + g)),
        pl.BlockSpec((rq, nstate), lambda b, g, s: (b * ns + chunk_of(s), bo + groups + g)),
        pl.BlockSpec((rq, heads), lambda b, g, s: (b * ns + chunk_of(s), dto + s // ns)),
        pl.BlockSpec((rq, gw), lambda b, g, s: (b * ns + out_chunk(s), g)),
        pl.BlockSpec((1, 1, heads), lambda b, g, s: (s // ns, 0, 0)),
        pl.BlockSpec((1, 1, heads), lambda b, g, s: (s // ns, 0, 0)),
        pl.BlockSpec((1, gw), lambda b, g, s: (0, g)),
        pl.BlockSpec((1, gw), lambda b, g, s: (0, g)),
        pl.BlockSpec((LANES, hpg * q), lambda b, g, s: (0, 0)),
    ]
    args = [xbc, xbc, xbc, rest, z, dt_bias, a_log, d_skip_x, norm_w, _head_spread_matrix(hpg, q)]
    state_spec = pl.BlockSpec((1, hpg, hd, nstate), lambda b, g, s: (b, g, 0, 0))
    if h0 is not None:
        in_specs += [state_spec, state_spec]
        args += [h0[0], h0[1]]
    state_shape = jax.ShapeDtypeStruct((nb, heads, hd, nstate), F32)
    return pl.pallas_call(
        functools.partial(_ssd_kernel, ns=ns, cps=cps, hpg=hpg, hd=hd, heads=heads, has_state=h0 is not None),
        grid=(nb, groups, 2 * ns),
        in_specs=in_specs,
        out_specs=[pl.BlockSpec((rq, gw), lambda b, g, s: (b * ns + out_chunk(s), g)), state_spec, state_spec],
        out_shape=[jax.ShapeDtypeStruct((nb * seq, d_ssd), BF16), state_shape, state_shape],
        scratch_shapes=[pltpu.VMEM((seq, gw), F32), pltpu.VMEM((nstate, gw), F32)],
        compiler_params=_cparams("parallel", "parallel", "arbitrary"),
        name="ssd",
    )(*args)


def _lru_dir(x, xb, wa_ref, wx_ref, ba_ref, bx_ref, lam_ref, h0_ref, reverse, seq, a_scr, u_scr):
    r = jax.nn.sigmoid(_dot(xb, wa_ref[0].astype(BF16)) + ba_ref[...])
    i = jax.nn.sigmoid(_dot(xb, wx_ref[0].astype(BF16)) + bx_ref[...])
    log_a = -LRU_C * jax.nn.softplus(-lam_ref[...]) * r
    a = jnp.exp(log_a)
    u = jnp.sqrt(-jnp.tanh(log_a) * (a * a + 1.0)) * (i * x)
    n = x.shape[0]
    first = seq - 1 if reverse else 0
    row = lax.broadcasted_iota(jnp.int32, x.shape, 0) % seq
    h0_rows = jnp.concatenate([jnp.broadcast_to(h0_ref[s], (seq, x.shape[1])) for s in range(n // seq)], axis=0)
    u = u + jnp.where(row == first, a * h0_rows, 0.0)
    return _linear_scan(a, u, reverse, seq, a_scr, u_scr)


def _lru_kernel(x_ref, cw_ref, cb_ref, waf_ref, wxf_ref, baf_ref, bxf_ref, lamf_ref,
                wab_ref, wxb_ref, bab_ref, bxb_ref, lamb_ref, h0f_ref, h0b_ref,
                hr_ref, sf_ref, sb_ref, af_scr, uf_scr, ab_scr, ub_scr, *, seq):
    x = _conv4(x_ref[...], cw_ref[...], cb_ref[...], seq)
    xb = x.astype(BF16)
    h_f = _lru_dir(x, xb, waf_ref, wxf_ref, baf_ref, bxf_ref, lamf_ref, h0f_ref, False, seq, af_scr, uf_scr)
    h_b = _lru_dir(x, xb, wab_ref, wxb_ref, bab_ref, bxb_ref, lamb_ref, h0b_ref, True, seq, ab_scr, ub_scr)
    hr_ref[...] = h_f + h_b
    for s in range(sf_ref.shape[0]):
        sf_ref[s] = h_f[(s + 1) * seq - 1:(s + 1) * seq, :]
        sb_ref[s] = h_b[s * seq:s * seq + 1, :]


def _lru_call(xsrc, col_off, p, h0f, h0b, nb, seq):
    blocks, bw, _ = p['lru_wa_fwd'].shape
    d = blocks * bw
    assert col_off % bw == 0
    co = col_off // bw
    spb = _pick(nb, max(1, 2048 // seq), 1)
    rows = spb * seq
    vec = pl.BlockSpec((1, bw), lambda b, j: (0, j))
    wblk = pl.BlockSpec((1, bw, bw), lambda b, j: (j, 0, 0))
    st = pl.BlockSpec((spb, 1, bw), lambda b, j: (b, 0, j))
    st_shape = jax.ShapeDtypeStruct((nb, 1, d), F32)
    return pl.pallas_call(
        functools.partial(_lru_kernel, seq=seq),
        grid=(nb // spb, blocks),
        in_specs=[pl.BlockSpec((rows, bw), lambda b, j: (b, co + j)),
                  pl.BlockSpec((4, bw), lambda b, j: (0, j)), vec,
                  wblk, wblk, vec, vec, vec, wblk, wblk, vec, vec, vec, st, st],
        out_specs=[pl.BlockSpec((rows, bw), lambda b, j: (b, j)), st, st],
        out_shape=[jax.ShapeDtypeStruct((nb * seq, d), F32), st_shape, st_shape],
        scratch_shapes=[pltpu.VMEM((bw // LANES, rows, LANES), F32)] * 4,
        compiler_params=_cparams("parallel", "parallel"),
        name="rglru",
    )(xsrc, p['conv_lru_w'], p['conv_lru_b'],
      p['lru_wa_fwd'], p['lru_wx_fwd'], p['lru_ba_fwd'], p['lru_bx_fwd'], p['lru_lam_fwd'],
      p['lru_wa_bwd'], p['lru_wx_bwd'], p['lru_ba_bwd'], p['lru_bx_bwd'], p['lru_lam_bwd'],
      h0f, h0b)


def _gate_mul_kernel(g_ref, h_ref, o_ref):
    o_ref[...] = (jax.nn.gelu(g_ref[...]) * h_ref[...]).astype(o_ref.dtype)


def _gate_mul_call(proj, gate_off, hr):
    t, d = hr.shape
    tm = _pick(t, 1024, SUBLANES)
    tn = _pick(d, MXU_DIM, LANES)
    assert gate_off % tn == 0
    go = gate_off // tn
    return pl.pallas_call(
        _gate_mul_kernel,
        grid=(t // tm, d // tn),
        in_specs=[pl.BlockSpec((tm, tn), lambda i, j: (i, go + j)),
                  pl.BlockSpec((tm, tn), lambda i, j: (i, j))],
        out_specs=pl.BlockSpec((tm, tn), lambda i, j: (i, j)),
        out_shape=jax.ShapeDtypeStruct((t, d), BF16),
        compiler_params=_cparams("parallel", "parallel"),
        name="lru_gate",
    )(proj, hr)


def _layer_norm(v, g, b):
    mu = jnp.mean(v, axis=-1, keepdims=True)
    var = jnp.mean(jnp.square(v - mu), axis=-1, keepdims=True)
    return (v - mu) * lax.rsqrt(var + LN_EPS) * g + b


def _ln_router_kernel(v_ref, g_ref, b_ref, sh_ref, sc_ref, wr_ref, x1_ref, h2_ref, aff_ref,
                      *, tm, base, rpm, n_exp):
    r = _mod_row(pl.program_id(0), tm, base, rpm)
    x1 = _layer_norm(v_ref[...], g_ref[...], b_ref[...])
    x1_ref[...] = x1
    h2 = (x1 * (1.0 + sc_ref[pl.ds(r, 1), :]) + sh_ref[pl.ds(r, 1), :]).astype(BF16)
    h2_ref[...] = h2
    logits = _dot(h2, wr_ref[...].astype(BF16))
    lane = lax.broadcasted_iota(jnp.int32, logits.shape, 1)
    logits = jnp.where(lane < n_exp, logits, -jnp.inf)
    e = jnp.exp(logits - jnp.max(logits, axis=-1, keepdims=True))
    aff_ref[...] = e / jnp.sum(e, axis=-1, keepdims=True)


def _ln_router_call(v, ln_g, ln_b, mod, base, rpm, q_shift, q_scale, w_router_pad, n_exp):
    t, d = v.shape
    tm = _pick(t, 256, SUBLANES)
    assert rpm % tm == 0
    row = pl.BlockSpec((tm, d), lambda i: (i, 0))
    vec = pl.BlockSpec((1, d), lambda i: (0, 0))
    return pl.pallas_call(
        functools.partial(_ln_router_kernel, tm=tm, base=base, rpm=rpm, n_exp=n_exp),
        grid=(t // tm,),
        in_specs=[row, vec, vec,
                  pl.BlockSpec((MOD_ROWS, d), lambda i: (0, q_shift)),
                  pl.BlockSpec((MOD_ROWS, d), lambda i: (0, q_scale)),
                  pl.BlockSpec((d, LANES), lambda i: (0, 0))],
        out_specs=[row, row, pl.BlockSpec((tm, LANES), lambda i: (i, 0))],
        out_shape=[jax.ShapeDtypeStruct((t, d), F32), jax.ShapeDtypeStruct((t, d), BF16),
                   jax.ShapeDtypeStruct((t, LANES), F32)],
        compiler_params=_cparams("parallel"),
        name="ln1_router",
    )(v, ln_g, ln_b, mod, mod, w_router_pad)


def _experts_per_step(n, n_exp):
    return n_exp if n <= 512 else 1


def _rank_kernel(aff_ref, rank_ref, arow_ref, afft_scr, *, n, eps):
    eb = pl.program_id(1)

    @pl.when(eb == 0)
    def _():
        afft_scr[...] = aff_ref[...].T

    rb = LANES
    t_row = lax.broadcasted_iota(jnp.int32, (rb, n), 1)
    lane = lax.broadcasted_iota(jnp.int32, (rb, LANES), 1)
    for idx in range(eps):
        e = eb * eps + idx
        a_row = afft_scr[pl.ds(e, 1), :]
        arow_ref[0, idx] = a_row

        def body(c, cnt, e=e, a_row=a_row):
            r0 = pl.multiple_of(c * rb, rb)
            blk = aff_ref[pl.ds(r0, rb), :]
            a_col = jnp.sum(jnp.where(lane == e, blk, 0.0), axis=1, keepdims=True)
            t_col = r0 + lax.broadcasted_iota(jnp.int32, (rb, n), 0)
            beats = jnp.where(a_col > a_row, 1.0,
                              jnp.where(a_col == a_row, jnp.where(t_col < t_row, 1.0, 0.0), 0.0))
            return cnt + jnp.sum(beats, axis=0, keepdims=True)

        rank_ref[0, idx] = lax.fori_loop(0, n // rb, body, jnp.zeros((1, n), F32))


def _rank_call(aff, nsets, n, n_exp):
    eps = _experts_per_step(n, n_exp)
    out = jax.ShapeDtypeStruct((nsets, n_exp, 1, n), F32)
    spec = pl.BlockSpec((1, eps, 1, n), lambda s, e: (s, e, 0, 0))
    return pl.pallas_call(
        functools.partial(_rank_kernel, n=n, eps=eps),
        grid=(nsets, n_exp // eps),
        in_specs=[pl.BlockSpec((n, LANES), lambda s, e: (s, 0))],
        out_specs=[spec, spec],
        out_shape=[out, out],
        scratch_shapes=[pltpu.VMEM((LANES, n), F32)],
        compiler_params=_cparams("parallel", "arbitrary"),
        name="ec_rank",
    )(aff)


def _gather_kernel(rank_ref, arow_ref, h_ref, hs_ref, gate_ref, *, cap, eps):
    n, d = h_ref.shape
    slot = lax.broadcasted_iota(jnp.int32, (cap, n), 0).astype(F32)
    sels = [rank_ref[0, idx] == slot for idx in range(eps)]
    onehot = jnp.concatenate([jnp.where(s, 1.0, 0.0) for s in sels], axis=0).astype(BF16)
    hs_ref[...] = _dot(onehot, h_ref[...]).reshape(eps, cap, d).astype(hs_ref.dtype)
    for idx in range(eps):
        gate = jnp.sum(jnp.where(sels[idx], arow_ref[0, idx], 0.0), axis=1, keepdims=True)
        gate_ref[idx] = jnp.broadcast_to(gate, (cap, LANES))


def _gather_call(rank, arow, h2, nsets, n, cap, n_exp):
    d = h2.shape[1]
    eps = _experts_per_step(n, n_exp)
    rspec = pl.BlockSpec((1, eps, 1, n), lambda s, e: (s, e, 0, 0))
    return pl.pallas_call(
        functools.partial(_gather_kernel, cap=cap, eps=eps),
        grid=(nsets, n_exp // eps),
        in_specs=[rspec, rspec, _resident((n, d), lambda s, e: (s, 0))],
        out_specs=[pl.BlockSpec((eps, cap, d), lambda s, e: (e, s, 0)),
                   pl.BlockSpec((eps, cap, LANES), lambda s, e: (e, s, 0))],
        out_shape=[jax.ShapeDtypeStruct((n_exp, nsets * cap, d), BF16),
                   jax.ShapeDtypeStruct((n_exp, nsets * cap, LANES), F32)],
        compiler_params=_cparams("parallel", "arbitrary"),
        name="ec_gather",
    )(rank, arow, h2)


def _dot_nt(a, b):
    return lax.dot_general(a, b, (((1,), (1,)), ((), ())), preferred_element_type=F32)


def _ffn_up_kernel(xp_ref, xs_ref, w1_ref, w3_ref, o_ref, *, d_ff, tn):
    w1 = w1_ref[0].astype(BF16)
    w3 = w3_ref[0].astype(BF16)
    mp = xp_ref.shape[1]
    j = pl.program_id(1)
    for x_ref, rows in ((xp_ref, slice(0, mp)), (xs_ref, slice(mp, None))):
        x = x_ref[0]
        u = _dot_nt(x, w1)
        v = _dot_nt(x, w3)
        col = j * tn + lax.broadcasted_iota(jnp.int32, u.shape, 1)
        o_ref[0, rows, :] = jnp.where(col < d_ff, jax.nn.silu(u) * v, 0.0).astype(o_ref.dtype)


def _ffn_up_call(hsel_p, hsel_s, w1t, w3t, f_pad):
    n_exp, mp, d = hsel_p.shape
    ms = hsel_s.shape[1]
    d_ff = w1t.shape[1]
    tn = MXU_DIM
    assert f_pad % tn == 0
    wspec = pl.BlockSpec((1, tn, d), lambda e, j: (e, j, 0))
    return pl.pallas_call(
        functools.partial(_ffn_up_kernel, d_ff=d_ff, tn=tn),
        grid=(n_exp, f_pad // tn),
        in_specs=[_resident((1, mp, d), lambda e, j: (e, 0, 0)),
                  _resident((1, ms, d), lambda e, j: (e, 0, 0)), wspec, wspec],
        out_specs=pl.BlockSpec((1, mp + ms, tn), lambda e, j: (e, 0, j)),
        out_shape=jax.ShapeDtypeStruct((n_exp, mp + ms, f_pad), BF16),
        compiler_params=_cparams("parallel", "arbitrary"),
        name="ffn_up",
    )(hsel_p, hsel_s, w1t, w3t)


def _ffn_down_kernel(a_ref, w_ref, gate_ref, hi_ref, lo_ref, acc_scr, *, kh, d_ff):
    k = pl.program_id(3)

    @pl.when(k == 0)
    def _():
        acc_scr[...] = _dot(a_ref[0, :, :kh], w_ref[0].astype(BF16))

    @pl.when(k == 1)
    def _():
        row = lax.broadcasted_iota(jnp.int32, w_ref.shape[1:], 0)
        w = jnp.where(row < d_ff - kh, w_ref[0], 0.0).astype(BF16)
        og = (acc_scr[...] + _dot(a_ref[0, :, kh:], w)) * gate_ref[0][:, :1]
        hi = og.astype(BF16)
        hi_ref[0] = hi
        lo_ref[0] = (og - hi.astype(F32)).astype(BF16)


def _ffn_down_call(a, w2, gate):
    n_exp, m, f_pad = a.shape
    d_ff, d = w2.shape[1:]
    kh = f_pad // 2
    assert kh % LANES == 0 and kh < d_ff <= f_pad
    tm = _pick(m, 1024, SUBLANES)
    tn = _pick(d, MXU_DIM, LANES)
    ospec = pl.BlockSpec((1, tm, tn), lambda e, i, j, k: (e, i, j))
    oshape = jax.ShapeDtypeStruct((n_exp, m, d), BF16)
    return pl.pallas_call(
        functools.partial(_ffn_down_kernel, kh=kh, d_ff=d_ff),
        grid=(n_exp, m // tm, d // tn, 2),
        in_specs=[_resident((1, tm, f_pad), lambda e, i, j, k: (e, i, 0)),
                  pl.BlockSpec((1, kh, tn), lambda e, i, j, k: (e, k, j)),
                  pl.BlockSpec((1, tm, LANES), lambda e, i, j, k: (e, i, 0))],
        out_specs=[ospec, ospec],
        out_shape=[oshape, oshape],
        scratch_shapes=[pltpu.VMEM((tm, tn), F32)],
        compiler_params=_cparams("parallel", "parallel", "arbitrary", "arbitrary"),
        name="ffn_down",
    )(a, w2, gate)


def _combine_kernel(rk_ref, hi_ref, lo_ref, x_ref, g2_ref, lg_ref, lb_ref, o_ref, acc_scr,
                    *, cap, ec, nk, base, rpm, tr, tiles_per_set, alpha):
    s = pl.program_id(0)
    i = pl.program_id(1)
    k = pl.program_id(2)
    kc = ec * cap
    d = x_ref.shape[1]
    rc = jnp.minimum(rk_ref[...], float(cap)).astype(BF16)
    src = lax.broadcasted_iota(jnp.int32, (LANES, kc), 0)
    col = lax.broadcasted_iota(jnp.int32, (LANES, kc), 1)
    expand = jnp.where(src == k * ec + col // cap, 1.0, 0.0).astype(BF16)
    r_exp = _dot(rc, expand)
    slot = (lax.broadcasted_iota(jnp.int32, (1, kc), 1) % cap).astype(F32)
    sel = jnp.where(r_exp == slot, 1.0, 0.0).astype(BF16)
    part = _dot(sel, hi_ref[...].reshape(kc, d)) + _dot(sel, lo_ref[...].reshape(kc, d))

    @pl.when(k == 0)
    def _():
        acc_scr[...] = part

    @pl.when(k > 0)
    def _():
        acc_scr[...] += part

    @pl.when(k == nk - 1)
    def _():
        r = _mod_row(s * tiles_per_set + i, tr, base, rpm)
        v = alpha * x_ref[...] + g2_ref[pl.ds(r, 1), :] * acc_scr[...]
        o_ref[...] = _layer_norm(v, lg_ref[...], lb_ref[...])


def _combine_call(rank_col, og_hi, og_lo, x1, mod, base, rpm, q_gate, ln_g, ln_b,
                  nsets, n, cap, slot_off, alpha):
    n_exp, _, d = og_hi.shape
    t = x1.shape[0]
    tr = _pick(n, 256, SUBLANES)
    tps = n // tr
    ec = max(1, min(n_exp, 512 // cap))
    nk = n_exp // ec
    assert slot_off % cap == 0 and n_exp % ec == 0 and rpm % tr == 0
    so = slot_off // cap
    ospec = pl.BlockSpec((ec, cap, d), lambda s, i, k: (k, so + s, 0))
    row = pl.BlockSpec((tr, d), lambda s, i, k: (s * tps + i, 0))
    vec = pl.BlockSpec((1, d), lambda s, i, k: (0, 0))
    return pl.pallas_call(
        functools.partial(_combine_kernel, cap=cap, ec=ec, nk=nk, base=base, rpm=rpm, tr=tr,
                          tiles_per_set=tps, alpha=alpha),
        grid=(nsets, tps, nk),
        in_specs=[pl.BlockSpec((tr, LANES), lambda s, i, k: (s * tps + i, 0)),
                  ospec, ospec, row,
                  pl.BlockSpec((MOD_ROWS, d), lambda s, i, k: (0, q_gate)), vec, vec],
        out_specs=row,
        out_shape=jax.ShapeDtypeStruct((t, d), F32),
        scratch_shapes=[pltpu.VMEM((tr, d), F32)],
        compiler_params=_cparams("parallel", "parallel", "arbitrary"),
        name="ec_combine",
    )(rank_col, og_hi, og_lo, x1, mod, ln_g, ln_b)


def _colmajor(x, nb, seq):
    d = x.shape[-1]
    return x.reshape(nb, seq // GRID_W, GRID_W, d).transpose(0, 2, 1, 3).reshape(nb * seq, d)


def _raster(x, nb, seq):
    d = x.shape[-1]
    return x.reshape(nb, GRID_W, seq // GRID_W, d).transpose(0, 2, 1, 3).reshape(nb * seq, d)


def _mixer_and_route(x, mod, base, rpm, p, states, is_latent, dims):
    nb, seq, d = x.shape
    t = nb * seq
    x2 = x.reshape(t, d)
    d_ssd, groups, hd, nstate, d_lru, n_exp, alpha = (dims[k] for k in
                                                      ('d_ssd', 'groups', 'hd', 'nstate', 'd_lru', 'n_exp', 'alpha'))
    d_xbc = d_ssd + 2 * groups * nstate
    heads = d_ssd // hd
    off_dt = d_ssd
    off_gate = off_dt + 2 * heads
    off_lx = off_gate + d_lru
    off_merge = off_lx + d_lru

    h = _modulate_call(x2, mod, base, rpm, 0, 1)
    proj = _proj_in_call(h, p['w_in'], d_ssd, d_xbc)
    xbc = _proj_in_conv_call(h, p['w_in'], d_ssd, p['conv_xbc_w'], p['conv_xbc_b'], seq)

    h0 = None if states is None else (states[0], states[1])
    y, s_ssd_f, s_ssd_b = _ssd_call(proj, xbc, proj, off_dt, p['dt_bias'], p['a_log'], p['d_skip_x'],
                                    p['ssd_norm_w'], nb, seq, d_ssd, groups, hd, nstate, h0)

    if states is None:
        h0f = h0b = jnp.zeros((nb, 1, d_lru), F32)
    else:
        h0f, h0b = states[2].reshape(nb, 1, d_lru), states[3].reshape(nb, 1, d_lru)
    if is_latent:
        lx = _colmajor(lax.slice_in_dim(proj, off_lx, off_lx + d_lru, axis=1), nb, seq)
        hr, s_lru_f, s_lru_b = _lru_call(lx, 0, p, h0f, h0b, nb, seq)
        hr = _raster(hr, nb, seq)
    else:
        hr, s_lru_f, s_lru_b = _lru_call(proj, off_lx, p, h0f, h0b, nb, seq)
    hl = _gate_mul_call(proj, off_gate, hr)

    branch_ssd = _proj_ssd_call(y, p['w_proj_ssd'], proj, off_merge)
    merged = _proj_lru_call(hl, p['w_proj_lru'], proj, off_merge + d, branch_ssd)
    v1 = _proj_out_call(merged, p['w_out'], x2, mod, base, rpm, 2, alpha)
    x1, h2, aff = _ln_router_call(v1, p['ln1_g'], p['ln1_b'], mod, base, rpm, 3, 4, p['w_router_pad'], n_exp)
    rank, arow = _rank_call(aff, nb, seq, n_exp)
    return x1, h2, rank, arow, (s_ssd_f, s_ssd_b, s_lru_f, s_lru_b)


def _rank_columns(rank, nsets, n, n_exp):
    rc = rank.reshape(nsets, n_exp, n).transpose(0, 2, 1).reshape(nsets * n, n_exp)
    return jnp.pad(rc, ((0, 0), (0, LANES - n_exp)), constant_values=float(n))


def kernel(x_prompt, x_sample, state_ssd_fwd, state_ssd_bwd, state_lru_fwd, state_lru_bwd, c, c_ctx,
           w_mod, b_mod, w_in, conv_xbc_w, conv_xbc_b, dt_bias_fwd, dt_bias_bwd, a_log_fwd, a_log_bwd,
           d_skip, ssd_norm_w, w_proj_ssd, conv_lru_w, conv_lru_b,
           lru_wa_fwd, lru_ba_fwd, lru_wx_fwd, lru_bx_fwd, lru_lam_fwd,
           lru_wa_bwd, lru_ba_bwd, lru_wx_bwd, lru_bx_bwd, lru_lam_bwd,
           w_proj_lru, w_out, ln1_g, ln1_b, w_router, w1, w3, w2, ln2_g, ln2_b):
    depth, d, _ = w_mod.shape
    bp, seq_p, _ = x_prompt.shape
    bs, seq_s, _ = x_sample.shape
    heads, hd, nstate = state_ssd_fwd.shape[2:]
    d_ssd = w_proj_ssd.shape[1]
    d_xbc = conv_xbc_w.shape[2]
    groups = (d_xbc - d_ssd) // (2 * nstate)
    d_lru = w_proj_lru.shape[1]
    n_exp, _, d_ff = w1.shape[1:]
    alpha = (2.0 * depth) ** 0.25
    dims = dict(d_ssd=d_ssd, groups=groups, hd=hd, nstate=nstate, d_lru=d_lru, n_exp=n_exp, alpha=alpha)
    assert heads * hd == d_ssd and n_exp <= LANES and bs + 1 <= MOD_ROWS and seq_s % GRID_W == 0
    cap_p = EC_CAPACITY_FACTOR * seq_p // n_exp
    cap_s = EC_CAPACITY_FACTOR * seq_s // n_exp
    rows_p, rows_s = bp * cap_p, bs * cap_s
    f_pad = -(-d_ff // (2 * LANES)) * (2 * LANES)

    cv = jnp.zeros((MOD_ROWS, d), F32).at[0].set(c_ctx).at[1:1 + bs].set(c)

    y_p, y_s = x_prompt, x_sample
    new_states = ([], [], [], [])
    for layer in range(depth):
        p = dict(
            w_in=w_in[layer], conv_xbc_w=conv_xbc_w[layer], conv_xbc_b=conv_xbc_b[layer][None],
            dt_bias=jnp.stack([dt_bias_fwd[layer], dt_bias_bwd[layer]])[:, None, :],
            a_log=jnp.stack([a_log_fwd[layer], a_log_bwd[layer]])[:, None, :],
            d_skip_x=jnp.repeat(d_skip[layer], hd)[None], ssd_norm_w=ssd_norm_w[layer][None],
            w_proj_ssd=w_proj_ssd[layer], conv_lru_w=conv_lru_w[layer], conv_lru_b=conv_lru_b[layer][None],
            lru_wa_fwd=lru_wa_fwd[layer], lru_wx_fwd=lru_wx_fwd[layer],
            lru_ba_fwd=lru_ba_fwd[layer][None], lru_bx_fwd=lru_bx_fwd[layer][None],
            lru_lam_fwd=lru_lam_fwd[layer][None],
            lru_wa_bwd=lru_wa_bwd[layer], lru_wx_bwd=lru_wx_bwd[layer],
            lru_ba_bwd=lru_ba_bwd[layer][None], lru_bx_bwd=lru_bx_bwd[layer][None],
            lru_lam_bwd=lru_lam_bwd[layer][None],
            w_proj_lru=w_proj_lru[layer], w_out=w_out[layer],
            ln1_g=ln1_g[layer][None], ln1_b=ln1_b[layer][None],
            w_router_pad=jnp.pad(w_router[layer], ((0, 0), (0, LANES - n_exp))),
        )
        mod = _mod_call(cv, w_mod[layer], b_mod[layer][None])

        x1_p, h2_p, rank_p, arow_p, st = _mixer_and_route(
            y_p, mod, 0, bp * seq_p, p, None, False, dims)
        cached = (state_ssd_fwd[:, layer], state_ssd_bwd[:, layer], state_lru_fwd[:, layer], state_lru_bwd[:, layer])
        x1_s, h2_s, rank_s, arow_s, _ = _mixer_and_route(
            y_s, mod, 1, seq_s, p, cached, True, dims)
        for acc, s in zip(new_states, st):
            acc.append(s)

        hsel_p, gate_p = _gather_call(rank_p, arow_p, h2_p, bp, seq_p, cap_p, n_exp)
        hsel_s, gate_s = _gather_call(rank_s, arow_s, h2_s, bs, seq_s, cap_s, n_exp)
        act = _ffn_up_call(hsel_p, hsel_s, jnp.swapaxes(w1[layer], 1, 2), jnp.swapaxes(w3[layer], 1, 2), f_pad)
        og_hi, og_lo = _ffn_down_call(act, w2[layer], jnp.concatenate([gate_p, gate_s], axis=1))

        g2, b2 = ln2_g[layer][None], ln2_b[layer][None]
        y_p = _combine_call(_rank_columns(rank_p, bp, seq_p, n_exp), og_hi, og_lo, x1_p, mod, 0, bp * seq_p, 5,
                            g2, b2, bp, seq_p, cap_p, 0, alpha).reshape(bp, seq_p, d)
        y_s = _combine_call(_rank_columns(rank_s, bs, seq_s, n_exp), og_hi, og_lo, x1_s, mod, 1, seq_s, 5,
                            g2, b2, bs, seq_s, cap_s, rows_p, alpha).reshape(bs, seq_s, d)

    new_ssd_fwd = jnp.stack(new_states[0], axis=1)
    new_ssd_bwd = jnp.stack(new_states[1], axis=1)
    new_lru_fwd = jnp.concatenate(new_states[2], axis=1)
    new_lru_bwd = jnp.concatenate(new_states[3], axis=1)
    return (y_p, y_s, new_ssd_fwd, new_ssd_bwd, new_lru_fwd, new_lru_bwd)
```
